```python
import jax, jax.numpy as jnp
from jax import lax
import numpy as np

D_MODEL = 1024
BATCH = 2
SEQ = 8192
DEPTH = 4
DEC_BATCH = 128
DEC_SEQ = 1
PAST_LEN = 2048
PAGE_SIZE = 128

N_HEADS = 16
HEAD_DIM = D_MODEL // N_HEADS
N_IDX_HEADS = 8
IDX_DIM = 64
TOPK_MAX = 256
Q_BLOCK = 128
CONV_W = 3
D_FF = (7 * D_MODEL) // 2
N_EXPERTS = 8
TOP_E = 2
EPS = 1e-6
N_ATTN_LAYERS = (DEPTH + 1) // 2
N_CONV_LAYERS = DEPTH // 2
ATTN_IN_COLS = 3 * D_MODEL + N_IDX_HEADS * IDX_DIM + IDX_DIM + N_IDX_HEADS

kernel_name = "hybrid_dsa_shortconv_moe_decoder_step"


def rmsnorm(x, g):
    xf = x.astype(jnp.float32)
    inv = lax.rsqrt(jnp.mean(xf * xf, axis=-1, keepdims=True) + EPS)
    return (xf * inv).astype(x.dtype) * g


def adaln_params(c, w, b):
    mod = jax.nn.silu(c) @ w + b
    return [m[:, None, :] for m in jnp.split(mod, 6, axis=-1)]


def modulate(h, shift, scale):
    return h * (1 + scale) + shift


def split_attn_proj(p):
    d = D_MODEL
    lead = p.shape[:-1]
    o_qi = 3 * d
    o_ki = o_qi + N_IDX_HEADS * IDX_DIM
    o_w = o_ki + IDX_DIM
    q = p[..., :d].reshape(*lead, N_HEADS, HEAD_DIM)
    k = p[..., d:2 * d].reshape(*lead, N_HEADS, HEAD_DIM)
    v = p[..., 2 * d:3 * d].reshape(*lead, N_HEADS, HEAD_DIM)
    q_idx = p[..., o_qi:o_ki].reshape(*lead, N_IDX_HEADS, IDX_DIM)
    k_idx = p[..., o_ki:o_w]
    w_idx = p[..., o_w:o_w + N_IDX_HEADS]
    return q, k, v, q_idx, k_idx, w_idx


def indexer_scores(q_idx, w_idx, k_idx):
    qk = jnp.einsum('bthd,bsd->bths', q_idx.astype(jnp.float32), k_idx.astype(jnp.float32)) * (IDX_DIM ** -0.5)
    w = w_idx.astype(jnp.float32) * (N_IDX_HEADS ** -0.5)
    return jnp.einsum('bth,bths->bts', w, jax.nn.relu(qk))


def sparse_attend(q, k_sel, v_sel, valid):
    logits = jnp.einsum('bthd,btkhd->bthk', q.astype(jnp.float32), k_sel.astype(jnp.float32)) * (HEAD_DIM ** -0.5)
    logits = jnp.where(valid[:, :, None, :], logits, -jnp.inf)
    p = jax.nn.softmax(logits, axis=-1)
    return jnp.einsum('bthk,btkhd->bthd', p.astype(v_sel.dtype), v_sel)


def mixer_a_prompt(h, w_in, w_out):
    B_, T_, _ = h.shape
    q, k, v, q_idx, k_idx, w_idx = split_attn_proj(h @ w_in)
    topk = min(TOPK_MAX, T_ // 4)
    key_pos = jnp.arange(T_)
    bidx = jnp.arange(B_)[:, None, None]

    def block(i):
        start = i * Q_BLOCK
        qb = lax.dynamic_slice_in_dim(q, start, Q_BLOCK, axis=1)
        qib = lax.dynamic_slice_in_dim(q_idx, start, Q_BLOCK, axis=1)
        wib = lax.dynamic_slice_in_dim(w_idx, start, Q_BLOCK, axis=1)
        scores = indexer_scores(qib, wib, k_idx)
        q_pos = start + jnp.arange(Q_BLOCK)
        scores = jnp.where(key_pos[None, None, :] <= q_pos[None, :, None], scores, -jnp.inf)
        top_s, top_i = lax.top_k(scores, topk)
        k_sel = k[bidx, top_i]
        v_sel = v[bidx, top_i]
        return sparse_attend(qb, k_sel, v_sel, jnp.isfinite(top_s))

    out = lax.map(block, jnp.arange(T_ // Q_BLOCK))
    out = jnp.transpose(out, (1, 0, 2, 3, 4)).reshape(B_, T_, D_MODEL)
    return out @ w_out, k, v, k_idx


def mixer_a_sample(h, cache_k, cache_v, cache_kidx, page_table, layer, w_in, w_out):
    DB, Tn, _ = h.shape
    q, k, v, q_idx, k_idx, w_idx = split_attn_proj(h @ w_in)
    L = PAST_LEN + Tn
    topk = min(TOPK_MAX, L // 4)
    past_kidx = cache_kidx[layer, page_table].reshape(DB, PAST_LEN, IDX_DIM)
    all_kidx = jnp.concatenate([past_kidx.astype(k_idx.dtype), k_idx], axis=1)
    scores = indexer_scores(q_idx, w_idx, all_kidx)
    q_pos = PAST_LEN + jnp.arange(Tn)
    key_pos = jnp.arange(L)
    scores = jnp.where(key_pos[None, None, :] <= q_pos[None, :, None], scores, -jnp.inf)
    top_s, top_i = lax.top_k(scores, topk)
    bidx = jnp.arange(DB)[:, None, None]
    in_past = top_i < PAST_LEN
    past_i = jnp.minimum(top_i, PAST_LEN - 1)
    phys = page_table[bidx, past_i // PAGE_SIZE]
    slot = past_i % PAGE_SIZE
    new_i = jnp.clip(top_i - PAST_LEN, 0, Tn - 1)
    k_sel = jnp.where(in_past[..., None, None], cache_k[layer, phys, slot].astype(k.dtype), k[bidx, new_i])
    v_sel = jnp.where(in_past[..., None, None], cache_v[layer, phys, slot].astype(v.dtype), v[bidx, new_i])
    out = sparse_attend(q, k_sel, v_sel, jnp.isfinite(top_s)).reshape(DB, Tn, D_MODEL)
    return out @ w_out, k, v, k_idx


def mixer_b(h, conv_state, w_in, w_conv, w_out):
    T_ = h.shape[1]
    gb, gc, xv = jnp.split(h @ w_in, 3, axis=-1)
    u = gc * xv
    u_ext = jnp.concatenate([conv_state.astype(u.dtype), u], axis=1)
    conv = u_ext[:, 0:T_] * w_conv[0]
    for j in range(1, CONV_W):
        conv = conv + u_ext[:, j:j + T_] * w_conv[j]
    return (gb * conv) @ w_out, u_ext[:, -(CONV_W - 1):]


def swiglu(h, w_gu, w_down):
    g, u = jnp.split(h @ w_gu, 2, axis=-1)
    return (jax.nn.silu(g) * u) @ w_down


def moe_swiglu(h, w_router, b_router, w_gu, w_down):
    logits = (h @ w_router).astype(jnp.float32) + b_router.astype(jnp.float32)
    top_l, top_e = lax.top_k(logits, TOP_E)
    gates = jax.nn.softmax(top_l, axis=-1)
    combine = jnp.sum(jax.nn.one_hot(top_e, N_EXPERTS, dtype=jnp.float32) * gates[..., None], axis=-2)
    combine = combine.astype(h.dtype)
    y = jnp.zeros_like(h)
    for e in range(N_EXPERTS):
        y = y + combine[..., e:e + 1] * swiglu(h, w_gu[e], w_down[e])
    return y


def setup_inputs(seed: int = 0) -> dict:
    key = jax.random.key(seed)
    ks = jax.random.split(key, 26)
    n_pages = PAST_LEN // PAGE_SIZE
    n_pool = (DEC_BATCH * n_pages * 5) // 4
    f32 = jnp.float32
    nrm = lambda k, shape, s: jax.random.normal(k, shape, f32) * s
    d = D_MODEL
    page_table = jax.random.permutation(ks[0], n_pool)[:DEC_BATCH * n_pages].astype(jnp.int32).reshape(DEC_BATCH, n_pages)
    return {
        "x_prompt": nrm(ks[1], (BATCH, SEQ, d), 1.0),
        "x_sample": nrm(ks[2], (DEC_BATCH, DEC_SEQ, d), 1.0),
        "cache_k": nrm(ks[3], (N_ATTN_LAYERS, n_pool, PAGE_SIZE, N_HEADS, HEAD_DIM), 1.0),
        "cache_v": nrm(ks[4], (N_ATTN_LAYERS, n_pool, PAGE_SIZE, N_HEADS, HEAD_DIM), 1.0),
        "cache_kidx": nrm(ks[5], (N_ATTN_LAYERS, n_pool, PAGE_SIZE, IDX_DIM), 1.0),
        "state_conv": nrm(ks[6], (N_CONV_LAYERS, DEC_BATCH, CONV_W - 1, d), 1.0),
        "page_table": page_table,
        "c_prompt": nrm(ks[7], (BATCH, d), 1.0),
        "c_sample": nrm(ks[8], (DEC_BATCH, d), 1.0),
        "w_ada": nrm(ks[9], (DEPTH, d, 6 * d), 0.5 * d ** -0.5),
        "b_ada": nrm(ks[10], (DEPTH, 6 * d), 0.02),
        "g_norm_mix": 1.0 + nrm(ks[11], (DEPTH, d), 0.01),
        "g_norm_ffn": 1.0 + nrm(ks[12], (DEPTH, d), 0.01),
        "g_norm_final": 1.0 + nrm(ks[13], (d,), 0.01),
        "w_attn_in": nrm(ks[14], (N_ATTN_LAYERS, d, ATTN_IN_COLS), d ** -0.5),
        "w_attn_out": nrm(ks[15], (N_ATTN_LAYERS, d, d), d ** -0.5),
        "w_conv_in": nrm(ks[16], (N_CONV_LAYERS, d, 3 * d), d ** -0.5),
        "w_conv": nrm(ks[17], (N_CONV_LAYERS, CONV_W, d), CONV_W ** -0.5),
        "w_conv_out": nrm(ks[18], (N_CONV_LAYERS, d, d), d ** -0.5),
        "w_ffn_gate_up": nrm(ks[19], (N_ATTN_LAYERS, d, 2 * D_FF), d ** -0.5),
        "w_ffn_down": nrm(ks[20], (N_ATTN_LAYERS, D_FF, d), D_FF ** -0.5),
        "w_router": nrm(ks[21], (N_CONV_LAYERS, d, N_EXPERTS), d ** -0.5),
        "b_router": nrm(ks[22], (N_CONV_LAYERS, N_EXPERTS), 0.01),
        "w_moe_gate_up": nrm(ks[23], (N_CONV_LAYERS, N_EXPERTS, d, 2 * D_FF), d ** -0.5),
        "w_moe_down": nrm(ks[24], (N_CONV_LAYERS, N_EXPERTS, D_FF, d), D_FF ** -0.5),
    }


def reference(x_prompt, x_sample, cache_k, cache_v, cache_kidx, state_conv, page_table, c_prompt, c_sample,
              w_ada, b_ada, g_norm_mix, g_norm_ffn, g_norm_final, w_attn_in, w_attn_out,
              w_conv_in, w_conv, w_conv_out, w_ffn_gate_up, w_ffn_down, w_router, b_router,
              w_moe_gate_up, w_moe_down):
    xp, xs = x_prompt, x_sample
    kp_l, vp_l, ip_l, cp_l = [], [], [], []
    ks_l, vs_l, is_l, cs_l = [], [], [], []
    for l in range(DEPTH):
        j = l // 2
        sh_mp, sc_mp, gt_mp, sh_fp, sc_fp, gt_fp = adaln_params(c_prompt, w_ada[l], b_ada[l])
        sh_ms, sc_ms, gt_ms, sh_fs, sc_fs, gt_fs = adaln_params(c_sample, w_ada[l], b_ada[l])
        hp = modulate(rmsnorm(xp, g_norm_mix[l]), sh_mp, sc_mp)
        hs = modulate(rmsnorm(xs, g_norm_mix[l]), sh_ms, sc_ms)
        if l % 2 == 0:
            yp, kp, vp, ip = mixer_a_prompt(hp, w_attn_in[j], w_attn_out[j])
            ys, kk, vv, ii = mixer_a_sample(hs, cache_k, cache_v, cache_kidx, page_table, j, w_attn_in[j], w_attn_out[j])
            kp_l.append(kp); vp_l.append(vp); ip_l.append(ip)
            ks_l.append(kk); vs_l.append(vv); is_l.append(ii)
        else:
            zero_state = jnp.zeros((xp.shape[0], CONV_W - 1, D_MODEL), xp.dtype)
            yp, sp = mixer_b(hp, zero_state, w_conv_in[j], w_conv[j], w_conv_out[j])
            ys, ss = mixer_b(hs, state_conv[j], w_conv_in[j], w_conv[j], w_conv_out[j])
            cp_l.append(sp); cs_l.append(ss)
        xp = xp + gt_mp * yp
        xs = xs + gt_ms * ys
        hp = modulate(rmsnorm(xp, g_norm_ffn[l]), sh_fp, sc_fp)
        hs = modulate(rmsnorm(xs, g_norm_ffn[l]), sh_fs, sc_fs)
        if l % 2 == 0:
            fp = swiglu(hp, w_ffn_gate_up[j], w_ffn_down[j])
            fs = swiglu(hs, w_ffn_gate_up[j], w_ffn_down[j])
        else:
            fp = moe_swiglu(hp, w_router[j], b_router[j], w_moe_gate_up[j], w_moe_down[j])
            fs = moe_swiglu(hs, w_router[j], b_router[j], w_moe_gate_up[j], w_moe_down[j])
        xp = xp + gt_fp * fp
        xs = xs + gt_fs * fs
    y_prompt = rmsnorm(xp, g_norm_final)
    y_sample = rmsnorm(xs, g_norm_final)
    return (y_prompt, y_sample,
            jnp.stack(kp_l), jnp.stack(vp_l), jnp.stack(ip_l), jnp.stack(cp_l),
            jnp.stack(ks_l), jnp.stack(vs_l), jnp.stack(is_l), jnp.stack(cs_l))
```

```python
import functools

import jax
import jax.numpy as jnp
from jax import lax
from jax.experimental import pallas as pl
from jax.experimental.pallas import tpu as pltpu

F32 = jnp.float32
BF16 = jnp.bfloat16
I32 = jnp.int32

EPS = 1e-6
TOPK_MAX = 256
TOP_E = 2
LANES = 128
NEG_BIAS = -1e30
INT_MIN = -(2 ** 31)
KEY_NEG_INF = -2139095041
KEY_POS_INF = 2139095040
VMEM_LIMIT = 56 * 1024 * 1024


def _cparams(n_axes):
    return pltpu.CompilerParams(dimension_semantics=("arbitrary",) * n_axes,
                                vmem_limit_bytes=VMEM_LIMIT)


def _norm_mod(x, g, sh, sc):
    inv = lax.rsqrt(jnp.mean(x * x, axis=-1, keepdims=True) + EPS)
    return ((x * inv) * g) * (1.0 + sc) + sh


def _silu(x):
    return x * (1.0 / (1.0 + jnp.exp(-x)))


def _float_key(s):
    bits = lax.bitcast_convert_type(s, I32)
    return jnp.where(bits < 0, bits ^ jnp.int32(0x7FFFFFFF), bits)


def _adaln_kernel(c_ref, w_ref, b_ref, o_ref):
    a = _silu(c_ref[...]).astype(BF16)
    o_ref[0] = jnp.dot(a, w_ref[0].astype(BF16), preferred_element_type=F32) + b_ref[0]


def _adaln(c_all, w_ada, b_ada):
    depth, d, n = w_ada.shape
    r = c_all.shape[0]
    tn = n // 4
    return pl.pallas_call(
        _adaln_kernel,
        grid=(depth, n // tn),
        in_specs=[pl.BlockSpec((r, d), lambda l, j: (0, 0)),
                  pl.BlockSpec((1, d, tn), lambda l, j: (l, 0, j)),
                  pl.BlockSpec((1, 1, tn), lambda l, j: (l, 0, j))],
        out_specs=pl.BlockSpec((1, r, tn), lambda l, j: (l, 0, j)),
        out_shape=jax.ShapeDtypeStruct((depth, r, n), F32),
        compiler_params=_cparams(2),
        name="adaln",
    )(c_all, w_ada, b_ada.reshape(depth, 1, n))


def _mod_spec(tm, d, per_row):
    if per_row:
        return pl.BlockSpec((1, tm, d), lambda g, i, *_: (g, i, 0))
    return pl.BlockSpec((1, 1, d), lambda g, i, *_: (g, 0, 0))


def _attn_in_kernel(x_ref, sh_ref, sc_ref, g_ref, wq_ref, wk_ref, wv_ref, wqi_ref, wkw_ref,
                    q_ref, k_ref, v_ref, kb_ref, vb_ref, qi_ref, kw_ref, *, q_scale, qi_scale):
    h = _norm_mod(x_ref[0], g_ref[...], sh_ref[0], sc_ref[0]).astype(BF16)
    q_ref[0] = (jnp.dot(h, wq_ref[...], preferred_element_type=F32) * q_scale).astype(BF16)
    k = jnp.dot(h, wk_ref[...], preferred_element_type=F32)
    k_ref[0] = k
    kb_ref[0] = k.astype(BF16)
    v = jnp.dot(h, wv_ref[...], preferred_element_type=F32)
    v_ref[0] = v
    vb_ref[0] = v.astype(BF16)
    qi_ref[0] = (jnp.dot(h, wqi_ref[...], preferred_element_type=F32) * qi_scale).astype(BF16)
    kw_ref[0] = jnp.dot(h, wkw_ref[...], preferred_element_type=F32)


def _attn_in(x, sh, sc, g, ws, tm, per_row, head_dim, idx_dim):
    G, T, d = x.shape
    wq, wk, wv, wqi, wkw = ws
    nqi = wqi.shape[1]
    nkw = wkw.shape[1]
    full = lambda a: pl.BlockSpec(a.shape, lambda g_, i: (0,) * a.ndim)
    row = lambda n: pl.BlockSpec((1, tm, n), lambda g_, i: (g_, i, 0))
    kern = functools.partial(_attn_in_kernel, q_scale=head_dim ** -0.5, qi_scale=idx_dim ** -0.5)
    return pl.pallas_call(
        kern,
        grid=(G, T // tm),
        in_specs=[row(d), _mod_spec(tm, d, per_row), _mod_spec(tm, d, per_row), full(g),
                  full(wq), full(wk), full(wv), full(wqi), full(wkw)],
        out_specs=[row(d), row(d), row(d), row(d), row(d), row(nqi), row(nkw)],
        out_shape=[jax.ShapeDtypeStruct((G, T, d), BF16),
                   jax.ShapeDtypeStruct((G, T, d), F32),
                   jax.ShapeDtypeStruct((G, T, d), F32),
                   jax.ShapeDtypeStruct((G, T, d), BF16),
                   jax.ShapeDtypeStruct((G, T, d), BF16),
                   jax.ShapeDtypeStruct((G, T, nqi), BF16),
                   jax.ShapeDtypeStruct((G, T, nkw), F32)],
        compiler_params=_cparams(2),
        name="attn_in",
    )(x, sh, sc, g, wq, wk, wv, wqi, wkw)


def _count_groups(key_sc, n_slabs, groups, row_group, preds):
    parts = []
    for r0, pred in zip(groups, preds):
        def body(c, cnt, r0=r0, pred=pred):
            kk = key_sc[c, r0:r0 + row_group, :]
            return cnt + jnp.where(pred(kk, c), 1, 0).astype(I32)
        parts.append(lax.fori_loop(0, n_slabs, body, jnp.zeros((row_group, LANES), I32)))
    return [jnp.sum(p, axis=1, keepdims=True) for p in parts]


def _select_bias(key_sc, bias_sc, n_slabs, rows, row_group, topk):
    groups = list(range(0, rows, row_group))
    ng = len(groups)
    lane = lax.broadcasted_iota(I32, (row_group, LANES), 1)
    zeros = tuple(jnp.zeros((row_group, LANES), I32) for _ in groups)

    def bit_body(i, prefixes):
        bit = lax.shift_left(jnp.int32(1), 31 - i)
        trials = [p | bit for p in prefixes]
        preds = [lambda kk, c, ts=t ^ jnp.int32(INT_MIN): kk >= ts for t in trials]
        cnts = _count_groups(key_sc, n_slabs, groups, row_group, preds)
        return tuple(jnp.where(cnts[g] >= topk, trials[g], prefixes[g]) for g in range(ng))

    prefixes = lax.fori_loop(0, 32, bit_body, zeros)
    thrs = [p ^ jnp.int32(INT_MIN) for p in prefixes]
    n_gt = _count_groups(key_sc, n_slabs, groups, row_group,
                         [lambda kk, c, t=t: kk > t for t in thrs])
    n_eq = _count_groups(key_sc, n_slabs, groups, row_group,
                         [lambda kk, c, t=t: kk == t for t in thrs])
    needs = [topk - n for n in n_gt]
    tie = jnp.int32(0)
    for g in range(ng):
        tie = jnp.maximum(tie, jnp.max(jnp.where(n_eq[g] > needs[g], 1, 0)))

    def tie_cut():
        def cut_body(i, cuts):
            bit = lax.shift_left(jnp.int32(1), 30 - i)
            trials = [p | bit for p in cuts]
            preds = [lambda kk, c, t=thrs[g], tr=trials[g]: (kk == t) & (c * LANES + lane < tr)
                     for g in range(ng)]
            cnts = _count_groups(key_sc, n_slabs, groups, row_group, preds)
            return tuple(jnp.where(cnts[g] < needs[g], trials[g], cuts[g]) for g in range(ng))
        return lax.fori_loop(0, 31, cut_body, zeros)

    cuts = lax.cond(tie > 0, tie_cut,
                    lambda: tuple(jnp.full((row_group, LANES), 2 ** 31 - 1, I32) for _ in groups))

    def bias_body(c, carry):
        col = c * LANES + lane
        for g, r0 in enumerate(groups):
            kk = key_sc[c, r0:r0 + row_group, :]
            sel = (kk > thrs[g]) | ((kk == thrs[g]) & (col <= cuts[g]))
            fin = (kk > KEY_NEG_INF) & (kk < KEY_POS_INF)
            bias_sc[c, r0:r0 + row_group, :] = jnp.where(sel & fin, 0.0, NEG_BIAS).astype(F32)
        return carry
    lax.fori_loop(0, n_slabs, bias_body, 0)


def _prompt_attn_kernel(q_ref, qi_ref, w_ref, kT_ref, k_ref, v_ref, o_ref,
                        key_sc, bias_sc, m_sc, l_sc, acc_sc,
                        *, qb, tk, topk, n_heads, head_dim, n_ih, idx_dim, w_scale, row_group):
    qi_blk = pl.program_id(1)
    kj = pl.program_id(2)
    nk = pl.num_programs(2)
    n_valid = (qi_blk + 1) * qb
    n_chunks = (n_valid + tk - 1) // tk
    spc = tk // LANES

    @pl.when(kj == 0)
    def _indexer():
        qi = qi_ref[0]
        w = w_ref[0] * w_scale
        wcols = [w[:, h:h + 1] for h in range(n_ih)]
        row = qi_blk * qb + lax.broadcasted_iota(I32, (qb, tk), 0)
        lane = lax.broadcasted_iota(I32, (qb, tk), 1)

        def chunk_body(c, carry):
            kT = kT_ref[0, c]
            s = None
            for h in range(n_ih):
                d = jnp.dot(qi[:, h * idx_dim:(h + 1) * idx_dim], kT, preferred_element_type=F32)
                t = jnp.maximum(d, 0.0) * wcols[h]
                s = t if s is None else s + t
            key = jnp.where(c * tk + lane <= row, _float_key(s), jnp.int32(INT_MIN))
            for j in range(spc):
                key_sc[c * spc + j] = key[:, j * LANES:(j + 1) * LANES]
            return carry
        lax.fori_loop(0, n_chunks, chunk_body, 0)
        _select_bias(key_sc, bias_sc, n_chunks * spc, qb, row_group, topk)
        m_sc[...] = jnp.full(m_sc.shape, NEG_BIAS, F32)
        l_sc[...] = jnp.zeros(l_sc.shape, F32)
        acc_sc[...] = jnp.zeros(acc_sc.shape, F32)

    @pl.when(kj * tk < n_valid)
    def _attend():
        bias = jnp.concatenate([bias_sc[kj * spc + j] for j in range(spc)], axis=1)
        for h in range(n_heads):
            hs = slice(h * head_dim, (h + 1) * head_dim)
            s = lax.dot_general(q_ref[0, :, hs], k_ref[0, :, hs], (((1,), (1,)), ((), ())),
                                preferred_element_type=F32) + bias
            m_prev = m_sc[h]
            m_new = jnp.maximum(m_prev, jnp.max(s, axis=1, keepdims=True))
            alpha = jnp.exp(m_prev - m_new)
            p = jnp.exp(s - jnp.tile(m_new, (1, spc)))
            l_sc[h] = alpha * l_sc[h] + jnp.sum(p, axis=1, keepdims=True)
            pv = jnp.dot(p.astype(BF16), v_ref[0, :, hs], preferred_element_type=F32)
            acc_sc[h] = acc_sc[h] * alpha[:, :head_dim] + pv
            m_sc[h] = m_new

    @pl.when(kj == nk - 1)
    def _finish():
        outs = [acc_sc[h] * (1.0 / l_sc[h])[:, :head_dim] for h in range(n_heads)]
        o_ref[0] = jnp.concatenate(outs, axis=1).astype(o_ref.dtype)


def _prompt_attn(q, qi, w, kT, kb, vb, *, qb, tk, topk, n_heads, n_ih):
    B, T, d = q.shape
    head_dim = d // n_heads
    idx_dim = qi.shape[2] // n_ih
    nq, nk = T // qb, T // tk
    assert qb >= topk and tk % LANES == 0 and qb <= tk
    kT = kT.reshape(B, idx_dim, nk, tk).transpose(0, 2, 1, 3)

    def kv_map(b, i, j):
        last = ((i + 1) * qb - 1) // tk
        return (b, jnp.minimum(j, last), 0)

    kern = functools.partial(_prompt_attn_kernel, qb=qb, tk=tk, topk=topk, n_heads=n_heads,
                             head_dim=head_dim, n_ih=n_ih, idx_dim=idx_dim,
                             w_scale=n_ih ** -0.5, row_group=64)
    return pl.pallas_call(
        kern,
        grid=(B, nq, nk),
        in_specs=[pl.BlockSpec((1, qb, d), lambda b, i, j: (b, i, 0)),
                  pl.BlockSpec((1, qb, qi.shape[2]), lambda b, i, j: (b, i, 0)),
                  pl.BlockSpec((1, qb, w.shape[2]), lambda b, i, j: (b, i, 0)),
                  pl.BlockSpec((1, nk, idx_dim, tk), lambda b, i, j: (b, 0, 0, 0)),
                  pl.BlockSpec((1, tk, d), kv_map),
                  pl.BlockSpec((1, tk, d), kv_map)],
        out_specs=pl.BlockSpec((1, qb, d), lambda b, i, j: (b, i, 0)),
        out_shape=jax.ShapeDtypeStruct((B, T, d), BF16),
        scratch_shapes=[pltpu.VMEM((T // LANES, qb, LANES), I32),
                        pltpu.VMEM((T // LANES, qb, LANES), F32),
                        pltpu.VMEM((n_heads, qb, LANES), F32),
                        pltpu.VMEM((n_heads, qb, LANES), F32),
                        pltpu.VMEM((n_heads, qb, head_dim), F32)],
        compiler_params=_cparams(3),
        name="prompt_attn",
    )(q, qi, w, kT, kb, vb)


def _sample_scores_kernel(pt_ref, qi_ref, w_ref, *rest, pg):
    o_ref = rest[pg]
    qi = qi_ref[0]
    w = w_ref[0]
    for p in range(pg):
        kp = rest[p][0, 0].astype(BF16)
        d = lax.dot_general(qi, kp, (((1,), (1,)), ((), ())), preferred_element_type=F32)
        o_ref[0, 0, p:p + 1, :] = jnp.sum(jnp.maximum(d, 0.0) * w, axis=0, keepdims=True)


def _sample_scores(page_table, qi3, w3, cache_kidx, layer, pg):
    DB, n_pages = page_table.shape
    _, _, page, idx_dim = cache_kidx.shape
    n_ih = qi3.shape[1]
    page_specs = [
        pl.BlockSpec((1, 1, page, idx_dim),
                     lambda b, j, pt, p=p: (layer, pt[b, j * pg + p], 0, 0))
        for p in range(pg)]
    grid_spec = pltpu.PrefetchScalarGridSpec(
        num_scalar_prefetch=1,
        grid=(DB, n_pages // pg),
        in_specs=[pl.BlockSpec((1, n_ih, idx_dim), lambda b, j, pt: (b, 0, 0)),
                  pl.BlockSpec((1, n_ih, 1), lambda b, j, pt: (b, 0, 0))] + page_specs,
        out_specs=pl.BlockSpec((1, 1, pg, page), lambda b, j, pt: (b, j, 0, 0)))
    out = pl.pallas_call(
        functools.partial(_sample_scores_kernel, pg=pg),
        grid_spec=grid_spec,
        out_shape=jax.ShapeDtypeStruct((DB, n_pages // pg, pg, page), F32),
        compiler_params=_cparams(2),
        name="sample_scores",
    )(page_table, qi3, w3, *([cache_kidx] * pg))
    return out.reshape(DB, n_pages * page)


def _sample_select_kernel(sp_ref, qi_ref, kn_ref, w_ref, bias_ref, key_sc, *, topk, n_ih, idx_dim,
                          w_scale, row_group):
    DB, past = sp_ref.shape
    n_past = past // LANES
    qi = qi_ref[...].astype(F32)
    kn = kn_ref[...]
    w = w_ref[...] * w_scale
    s_new = None
    for h in range(n_ih):
        d = jnp.sum(qi[:, h * idx_dim:(h + 1) * idx_dim] * kn, axis=1, keepdims=True)
        t = jnp.maximum(d, 0.0) * w[:, h:h + 1]
        s_new = t if s_new is None else s_new + t
    for c in range(n_past):
        key_sc[c] = _float_key(sp_ref[:, c * LANES:(c + 1) * LANES])
    lane = lax.broadcasted_iota(I32, (DB, LANES), 1)
    key_sc[n_past] = jnp.where(lane == 0, _float_key(jnp.broadcast_to(s_new, (DB, LANES))),
                               jnp.int32(INT_MIN))
    _select_bias(key_sc, bias_ref, n_past + 1, DB, row_group, topk)


def _sample_select(scores_past, qi, kn, w, topk, n_ih):
    DB, past = scores_past.shape
    idx_dim = kn.shape[1]
    n_slabs = past // LANES + 1
    kern = functools.partial(_sample_select_kernel, topk=topk, n_ih=n_ih, idx_dim=idx_dim,
                             w_scale=n_ih ** -0.5, row_group=min(64, DB))
    return pl.pallas_call(
        kern,
        out_shape=jax.ShapeDtypeStruct((n_slabs, DB, LANES), F32),
        scratch_shapes=[pltpu.VMEM((n_slabs, DB, LANES), I32)],
        compiler_params=pltpu.CompilerParams(vmem_limit_bytes=VMEM_LIMIT),
        name="sample_select",
    )(scores_past, qi, kn, w)


def _sample_attn_kernel(pt_ref, q_ref, kn_ref, vn_ref, bn_ref, bias_ref, *rest, pg, n_heads, head_dim):
    k_refs = rest[:pg]
    v_refs = rest[pg:2 * pg]
    o_ref = rest[2 * pg]
    qblk_sc, m_sc, l_sc, acc_sc = rest[2 * pg + 1:]
    jg = pl.program_id(1)
    d = n_heads * head_dim
    page = k_refs[0].shape[2]
    diag = (lax.broadcasted_iota(I32, (n_heads, d), 1) // head_dim
            == lax.broadcasted_iota(I32, (n_heads, d), 0))

    @pl.when(jg == 0)
    def _init():
        qblk = jnp.where(diag, jnp.broadcast_to(q_ref[0].astype(F32), (n_heads, d)), 0.0)
        qblk_sc[...] = qblk.astype(BF16)
        s_new = jnp.sum(qblk * kn_ref[0], axis=1, keepdims=True) + bn_ref[0][:, 0:1]
        m_sc[...] = jnp.broadcast_to(s_new, m_sc.shape)
        l_sc[...] = jnp.ones(l_sc.shape, F32)
        acc_sc[...] = jnp.broadcast_to(vn_ref[0], acc_sc.shape)

    qblk = qblk_sc[...]
    for p in range(pg):
        kp = k_refs[p][0, 0].astype(BF16)
        s = lax.dot_general(qblk, kp, (((1,), (1,)), ((), ())), preferred_element_type=F32)
        s = s + bias_ref[0][:, p * page:(p + 1) * page]
        m_prev = m_sc[...]
        m_new = jnp.maximum(m_prev, jnp.max(s, axis=1, keepdims=True))
        alpha = jnp.exp(m_prev - m_new)
        pr = jnp.exp(s - jnp.tile(m_new, (1, page // LANES)))
        l_sc[...] = alpha * l_sc[...] + jnp.sum(pr, axis=1, keepdims=True)
        pv = jnp.dot(pr.astype(BF16), v_refs[p][0, 0].astype(BF16), preferred_element_type=F32)
        acc_sc[...] = acc_sc[...] * jnp.tile(alpha, (1, d // LANES)) + pv
        m_sc[...] = m_new

    @pl.when(jg == pl.num_programs(1) - 1)
    def _finish():
        inv = jnp.tile(1.0 / l_sc[...], (1, d // LANES))
        out = jnp.sum(jnp.where(diag, acc_sc[...] * inv, 0.0), axis=0, keepdims=True)
        o_ref[0] = out.astype(o_ref.dtype)


def _sample_attn(page_table, q, kn, vn, bias, cache_k, cache_v, layer, pg, n_heads):
    DB, n_pages = page_table.shape
    _, _, page, d = cache_k.shape
    head_dim = d // n_heads
    n_slabs = bias.shape[0]
    past = n_pages * page
    assert page % LANES == 0 and n_slabs * LANES == past + LANES
    bias_rows = bias.transpose(1, 0, 2).reshape(DB, 1, n_slabs * LANES)
    bias_new = bias_rows[:, :, past:]
    blk = pg * page
    page_spec = lambda p: pl.BlockSpec(
        (1, 1, page, d), lambda b, j, pt, p=p: (layer, pt[b, j * pg + p], 0, 0))
    row = lambda n: pl.BlockSpec((1, 1, n), lambda b, j, pt: (b, 0, 0))
    grid_spec = pltpu.PrefetchScalarGridSpec(
        num_scalar_prefetch=1,
        grid=(DB, n_pages // pg),
        in_specs=[row(d), row(d), row(d), row(LANES),
                  pl.BlockSpec((1, 1, blk), lambda b, j, pt: (b, 0, j))]
                 + [page_spec(p) for p in range(pg)] * 2,
        out_specs=row(d),
        scratch_shapes=[pltpu.VMEM((n_heads, d), BF16),
                        pltpu.VMEM((n_heads, LANES), F32),
                        pltpu.VMEM((n_heads, LANES), F32),
                        pltpu.VMEM((n_heads, d), F32)])
    return pl.pallas_call(
        functools.partial(_sample_attn_kernel, pg=pg, n_heads=n_heads, head_dim=head_dim),
        grid_spec=grid_spec,
        out_shape=jax.ShapeDtypeStruct((DB, 1, d), BF16),
        compiler_params=_cparams(2),
        name="sample_attn",
    )(page_table, q, kn, vn, bias_new, bias_rows, *([cache_k] * pg), *([cache_v] * pg))


def _proj_res_kernel(z_ref, w_ref, x_ref, gt_ref, o_ref):
    y = jnp.dot(z_ref[0], w_ref[...], preferred_element_type=F32)
    o_ref[0] = x_ref[0] + gt_ref[0] * y


def _proj_res(z, w, x, gt, tm, per_row):
    G, T, d = x.shape
    row = pl.BlockSpec((1, tm, d), lambda g, i: (g, i, 0))
    return pl.pallas_call(
        _proj_res_kernel,
        grid=(G, T // tm),
        in_specs=[row, pl.BlockSpec(w.shape, lambda g, i: (0, 0)), row, _mod_spec(tm, d, per_row)],
        out_specs=row,
        out_shape=jax.ShapeDtypeStruct((G, T, d), F32),
        compiler_params=_cparams(2),
        name="proj_res",
    )(z, w, x, gt)


def _conv_prompt_kernel(x_ref, sh_ref, sc_ref, gt_ref, g_ref, wb_ref, wc_ref, wx_ref, wcv_ref, wo_ref,
                        o_ref, st_ref, carry_sc, *, conv_w):
    i = pl.program_id(1)
    tm = x_ref.shape[1]
    x = x_ref[0]
    h = _norm_mod(x, g_ref[...], sh_ref[0], sc_ref[0]).astype(BF16)
    gb = jnp.dot(h, wb_ref[...], preferred_element_type=F32)
    u = (jnp.dot(h, wc_ref[...], preferred_element_type=F32)
         * jnp.dot(h, wx_ref[...], preferred_element_type=F32))

    @pl.when(i == 0)
    def _zero_state():
        carry_sc[...] = jnp.zeros(carry_sc.shape, F32)

    rows = lax.broadcasted_iota(I32, u.shape, 0)
    conv = u * wcv_ref[conv_w - 1:conv_w, :]
    for back in range(1, conv_w):
        shifted = pltpu.roll(u, back, axis=0)
        for r in range(back):
            prev = carry_sc[conv_w - 1 - back + r:conv_w - back + r, :]
            shifted = jnp.where(rows == r, prev, shifted)
        conv = conv + shifted * wcv_ref[conv_w - 1 - back:conv_w - back, :]
    carry_sc[...] = u[tm - (conv_w - 1):, :]
    st_ref[0] = u[tm - (conv_w - 1):, :]
    y = jnp.dot((gb * conv).astype(BF16), wo_ref[...], preferred_element_type=F32)
    o_ref[0] = x + gt_ref[0] * y


def _conv_prompt(x, sh, sc, gt, g, wb, wc, wx, wcv, wo, tm):
    B, T, d = x.shape
    conv_w = wcv.shape[0]
    row = pl.BlockSpec((1, tm, d), lambda b, i: (b, i, 0))
    full = lambda a: pl.BlockSpec(a.shape, lambda b, i: (0,) * a.ndim)
    mod = _mod_spec(tm, d, False)
    return pl.pallas_call(
        functools.partial(_conv_prompt_kernel, conv_w=conv_w),
        grid=(B, T // tm),
        in_specs=[row, mod, mod, mod, full(g), full(wb), full(wc), full(wx), full(wcv), full(wo)],
        out_specs=[row, pl.BlockSpec((1, conv_w - 1, d), lambda b, i: (b, 0, 0))],
        out_shape=[jax.ShapeDtypeStruct((B, T, d), F32),
                   jax.ShapeDtypeStruct((B, conv_w - 1, d), F32)],
        scratch_shapes=[pltpu.VMEM((conv_w - 1, d), F32)],
        compiler_params=_cparams(2),
        name="conv_prompt",
    )(x, sh, sc, gt, g, wb, wc, wx, wcv, wo)


def _conv_sample_kernel(x_ref, sh_ref, sc_ref, gt_ref, g_ref, st_ref, wb_ref, wc_ref, wx_ref, wcv_ref,
                        wo_ref, o_ref, ns_ref, *, conv_w):
    x = x_ref[...]
    h = _norm_mod(x, g_ref[...], sh_ref[...], sc_ref[...]).astype(BF16)
    gb = jnp.dot(h, wb_ref[...], preferred_element_type=F32)
    u = (jnp.dot(h, wc_ref[...], preferred_element_type=F32)
         * jnp.dot(h, wx_ref[...], preferred_element_type=F32))
    conv = u * wcv_ref[conv_w - 1:conv_w, :]
    for j in range(conv_w - 1):
        conv = conv + st_ref[j] * wcv_ref[j:j + 1, :]
    for j in range(conv_w - 2):
        ns_ref[j] = st_ref[j + 1]
    ns_ref[conv_w - 2] = u
    y = jnp.dot((gb * conv).astype(BF16), wo_ref[...], preferred_element_type=F32)
    o_ref[...] = x + gt_ref[...] * y


def _conv_sample(x, sh, sc, gt, g, state, wb, wc, wx, wcv, wo):
    DB, d = x.shape
    conv_w = wcv.shape[0]
    return pl.pallas_call(
        functools.partial(_conv_sample_kernel, conv_w=conv_w),
        out_shape=[jax.ShapeDtypeStruct((DB, d), F32),
                   jax.ShapeDtypeStruct((conv_w - 1, DB, d), F32)],
        compiler_params=pltpu.CompilerParams(vmem_limit_bytes=VMEM_LIMIT),
        name="conv_sample",
    )(x, sh, sc, gt, g, state, wb, wc, wx, wcv, wo)


def _norm_kernel(x_ref, sh_ref, sc_ref, g_ref, h_ref):
    h_ref[0] = _norm_mod(x_ref[0], g_ref[...], sh_ref[0], sc_ref[0]).astype(BF16)


def _norm_router_kernel(x_ref, sh_ref, sc_ref, g_ref, wr_ref, br_ref, h_ref, cb_ref, *, n_experts):
    h = _norm_mod(x_ref[0], g_ref[...], sh_ref[0], sc_ref[0])
    h_ref[0] = h.astype(BF16)
    logits = jnp.dot(h, wr_ref[...], preferred_element_type=F32,
                     precision=lax.Precision.HIGHEST) + br_ref[...]
    lane = lax.broadcasted_iota(I32, logits.shape, 1)
    ninf = jnp.float32(-jnp.inf)
    logits = jnp.where(lane < n_experts, logits, ninf)
    m1 = jnp.max(logits, axis=1, keepdims=True)
    i1 = jnp.min(jnp.where(logits == m1, lane, LANES), axis=1, keepdims=True)
    rest = jnp.where(lane == i1, ninf, logits)
    m2 = jnp.max(rest, axis=1, keepdims=True)
    i2 = jnp.min(jnp.where(rest == m2, lane, LANES), axis=1, keepdims=True)
    e2 = jnp.exp(m2 - m1)
    inv = 1.0 / (1.0 + e2)
    cb_ref[0] = jnp.where(lane == i1, inv, jnp.where(lane == i2, e2 * inv, 0.0))


def _norm(x, sh, sc, g, tm, per_row, router=None):
    G, T, d = x.shape
    row = pl.BlockSpec((1, tm, d), lambda g_, i: (g_, i, 0))
    full = lambda a: pl.BlockSpec(a.shape, lambda g_, i: (0,) * a.ndim)
    mod = _mod_spec(tm, d, per_row)
    if router is None:
        return pl.pallas_call(
            _norm_kernel, grid=(G, T // tm),
            in_specs=[row, mod, mod, full(g)], out_specs=row,
            out_shape=jax.ShapeDtypeStruct((G, T, d), BF16),
            compiler_params=_cparams(2), name="norm",
        )(x, sh, sc, g)
    wr, br, n_experts = router
    return pl.pallas_call(
        functools.partial(_norm_router_kernel, n_experts=n_experts), grid=(G, T // tm),
        in_specs=[row, mod, mod, full(g), full(wr), full(br)],
        out_specs=[row, pl.BlockSpec((1, tm, LANES), lambda g_, i: (g_, i, 0))],
        out_shape=[jax.ShapeDtypeStruct((G, T, d), BF16),
                   jax.ShapeDtypeStruct((G, T, LANES), F32)],
        compiler_params=_cparams(2), name="norm_router",
    )(x, sh, sc, g, wr, br)


def _ffn_kernel(*refs, has_combine):
    if has_combine:
        h_ref, x_ref, gt_ref, cb_ref, wg_ref, wu_ref, wd_ref, o_ref, acc_sc = refs
    else:
        h_ref, x_ref, gt_ref, wg_ref, wu_ref, wd_ref, o_ref, acc_sc = refs
    e = pl.program_id(2)
    f = pl.program_id(3)

    @pl.when((e == 0) & (f == 0))
    def _zero():
        acc_sc[...] = jnp.zeros(acc_sc.shape, F32)

    h = h_ref[0]
    gate = jnp.dot(h, wg_ref[0].astype(BF16), preferred_element_type=F32)
    up = jnp.dot(h, wu_ref[0].astype(BF16), preferred_element_type=F32)
    a = _silu(gate) * up
    y = jnp.dot(a.astype(BF16), wd_ref[0].astype(BF16), preferred_element_type=F32)
    if has_combine:
        cb = cb_ref[0]
        lane = lax.broadcasted_iota(I32, cb.shape, 1)
        y = y * jnp.sum(jnp.where(lane == e, cb, 0.0), axis=1, keepdims=True)
    acc_sc[...] += y

    @pl.when((e == pl.num_programs(2) - 1) & (f == pl.num_programs(3) - 1))
    def _finish():
        o_ref[0] = x_ref[0] + gt_ref[0] * acc_sc[...]


def _ffn(h, x, gt, combine, w_gu, w_down, e_off, n_e, tm, tf, per_row):
    G, T, d = x.shape
    d_ff = w_gu.shape[2] // 2
    nf = d_ff // tf
    row = pl.BlockSpec((1, tm, d), lambda g, i, e, f: (g, i, 0))
    in_specs = [row, row, _mod_spec(tm, d, per_row)]
    args = [h, x, gt]
    if combine is not None:
        in_specs.append(pl.BlockSpec((1, tm, LANES), lambda g, i, e, f: (g, i, 0)))
        args.append(combine)
    in_specs += [pl.BlockSpec((1, d, tf), lambda g, i, e, f: (e_off + e, 0, f)),
                 pl.BlockSpec((1, d, tf), lambda g, i, e, f: (e_off + e, 0, nf + f)),
                 pl.BlockSpec((1, tf, d), lambda g, i, e, f: (e_off + e, f, 0))]
    args += [w_gu, w_gu, w_down]
    return pl.pallas_call(
        functools.partial(_ffn_kernel, has_combine=combine is not None),
        grid=(G, T // tm, n_e, nf),
        in_specs=in_specs,
        out_specs=row,
        out_shape=jax.ShapeDtypeStruct((G, T, d), F32),
        scratch_shapes=[pltpu.VMEM((tm, d), F32)],
        compiler_params=_cparams(4),
        name="ffn",
    )(*args)


def _final_norm_kernel(x_ref, g_ref, o_ref):
    x = x_ref[0]
    inv = lax.rsqrt(jnp.mean(x * x, axis=-1, keepdims=True) + EPS)
    o_ref[0] = (x * inv) * g_ref[...]


def _final_norm(x, g, tm):
    G, T, d = x.shape
    row = pl.BlockSpec((1, tm, d), lambda g_, i: (g_, i, 0))
    return pl.pallas_call(
        _final_norm_kernel, grid=(G, T // tm),
        in_specs=[row, pl.BlockSpec(g.shape, lambda g_, i: (0, 0))], out_specs=row,
        out_shape=jax.ShapeDtypeStruct((G, T, d), F32),
        compiler_params=_cparams(2), name="final_norm",
    )(x, g)


def _largest_tile(n, cap):
    t = min(n, cap)
    while n % t:
        t //= 2
    return t


def kernel(x_prompt, x_sample, cache_k, cache_v, cache_kidx, state_conv, page_table, c_prompt, c_sample,
           w_ada, b_ada, g_norm_mix, g_norm_ffn, g_norm_final, w_attn_in, w_attn_out,
           w_conv_in, w_conv, w_conv_out, w_ffn_gate_up, w_ffn_down, w_router, b_router,
           w_moe_gate_up, w_moe_down):
    B, T, d = x_prompt.shape
    DB, Tn, _ = x_sample.shape
    assert Tn == 1
    depth = w_ada.shape[0]
    n_attn, n_pool, page, n_heads, head_dim = cache_k.shape
    idx_dim = cache_kidx.shape[-1]
    n_ih = (w_attn_in.shape[-1] - 3 * d - idx_dim) // (idx_dim + 1)
    n_pages = page_table.shape[1]
    past = n_pages * page
    n_experts = w_router.shape[-1]
    conv_w = w_conv.shape[1]
    topk_p = min(TOPK_MAX, T // 4)
    topk_s = min(TOPK_MAX, (past + Tn) // 4)

    tm_p = _largest_tile(T, 512)
    tm_f = _largest_tile(T, 1024)
    tf = _largest_tile(w_ffn_gate_up.shape[-1] // 2, 512)
    pg = _largest_tile(n_pages, 8)
    tk = _largest_tile(T, 512)
    qb = _largest_tile(T, 256)

    r = B + DB
    r_pad = -(-r // 8) * 8
    c_all = jnp.concatenate([c_prompt, c_sample, jnp.zeros((r_pad - r, d), F32)], axis=0)
    mod = _adaln(c_all, w_ada, b_ada)

    w_moe_gu = w_moe_gate_up.reshape((-1,) + w_moe_gate_up.shape[2:])
    w_moe_dn = w_moe_down.reshape((-1,) + w_moe_down.shape[2:])
    ck = cache_k.reshape(n_attn, n_pool, page, d)
    cv = cache_v.reshape(n_attn, n_pool, page, d)
    xp = x_prompt
    xs = x_sample.reshape(1, DB, d)

    outs = {n: [] for n in ("kp", "vp", "ip", "cp", "ks", "vs", "is", "cs")}
    for l in range(depth):
        j = l // 2
        mp = [mod[l, :B, i * d:(i + 1) * d].reshape(B, 1, d) for i in range(6)]
        ms = [mod[l, B:r, i * d:(i + 1) * d].reshape(1, DB, d) for i in range(6)]
        g_mix = g_norm_mix[l].reshape(1, d)
        g_ffn = g_norm_ffn[l].reshape(1, d)
        if l % 2 == 0:
            wi = w_attn_in[j].astype(BF16)
            nkw = idx_dim + n_ih
            wkw = jnp.pad(wi[:, 3 * d + n_ih * idx_dim:], ((0, 0), (0, LANES - nkw)))
            ws = (wi[:, :d], wi[:, d:2 * d], wi[:, 2 * d:3 * d],
                  wi[:, 3 * d:3 * d + n_ih * idx_dim], wkw)
            wo = w_attn_out[j].astype(BF16)
            q, k, v, kb, vb, qi, kw = _attn_in(xp, mp[0], mp[1], g_mix, ws, tm_p, False, head_dim, idx_dim)
            kidx = kw[:, :, :idx_dim]
            widx = kw[:, :, idx_dim:nkw]
            kT = jnp.swapaxes(kidx.astype(BF16), 1, 2)
            att = _prompt_attn(q, qi, widx, kT, kb, vb, qb=qb, tk=tk, topk=topk_p,
                               n_heads=n_heads, n_ih=n_ih)
            xp = _proj_res(att, wo, xp, mp[2], tm_p, False)
            outs["kp"].append(k.reshape(B, T, n_heads, head_dim))
            outs["vp"].append(v.reshape(B, T, n_heads, head_dim))
            outs["ip"].append(kidx)
            q, k, v, _, _, qi, kw = _attn_in(xs, ms[0], ms[1], g_mix, ws, DB, True, head_dim, idx_dim)
            kidx = kw[0, :, :idx_dim]
            widx = kw[0, :, idx_dim:nkw]
            sc_past = _sample_scores(page_table, qi.reshape(DB, n_ih, idx_dim),
                                     (widx * n_ih ** -0.5).reshape(DB, n_ih, 1), cache_kidx, j, pg)
            bias = _sample_select(sc_past, qi[0], kidx, widx, topk_s, n_ih)
            att = _sample_attn(page_table, q.reshape(DB, 1, d), k.reshape(DB, 1, d),
                               v.reshape(DB, 1, d), bias, ck, cv, j, pg, n_heads)
            xs = _proj_res(att.reshape(1, DB, d), wo, xs, ms[2], DB, True)
            outs["ks"].append(k.reshape(DB, Tn, n_heads, head_dim))
            outs["vs"].append(v.reshape(DB, Tn, n_heads, head_dim))
            outs["is"].append(kidx.reshape(DB, Tn, idx_dim))
            hp = _norm(xp, mp[3], mp[4], g_ffn, tm_p, False)
            xp = _ffn(hp, xp, mp[5], None, w_ffn_gate_up, w_ffn_down, j, 1, tm_f, tf, False)
            hs = _norm(xs, ms[3], ms[4], g_ffn, DB, True)
            xs = _ffn(hs, xs, ms[5], None, w_ffn_gate_up, w_ffn_down, j, 1, DB, tf, True)
        else:
            wi = w_conv_in[j].astype(BF16)
            wb, wc, wx = wi[:, :d], wi[:, d:2 * d], wi[:, 2 * d:]
            wo = w_conv_out[j].astype(BF16)
            xp, st = _conv_prompt(xp, mp[0], mp[1], mp[2], g_mix, wb, wc, wx, w_conv[j], wo, tm_p)
            outs["cp"].append(st)
            xs2, ns = _conv_sample(xs[0], ms[0][0], ms[1][0], ms[2][0], g_mix,
                                   jnp.swapaxes(state_conv[j], 0, 1), wb, wc, wx, w_conv[j], wo)
            xs = xs2.reshape(1, DB, d)
            outs["cs"].append(jnp.swapaxes(ns, 0, 1))
            wr = jnp.pad(w_router[j], ((0, 0), (0, LANES - n_experts)))
            br = jnp.pad(b_router[j], (0, LANES - n_experts)).reshape(1, LANES)
            router = (wr, br, n_experts)
            hp, cb = _norm(xp, mp[3], mp[4], g_ffn, tm_p, False, router)
            xp = _ffn(hp, xp, mp[5], cb, w_moe_gu, w_moe_dn, j * n_experts, n_experts, tm_f, tf, False)
            hs, cb = _norm(xs, ms[3], ms[4], g_ffn, DB, True, router)
            xs = _ffn(hs, xs, ms[5], cb, w_moe_gu, w_moe_dn, j * n_experts, n_experts, DB, tf, True)

    gf = g_norm_final.reshape(1, d)
    y_prompt = _final_norm(xp, gf, tm_p)
    y_sample = _final_norm(xs, gf, DB).reshape(DB, Tn, d)
    st = lambda n: jnp.stack(outs[n])
    return (y_prompt, y_sample, st("kp"), st("vp"), st("ip"), st("cp"),
            st("ks"), st("vs"), st("is"), st("cs"))
```

```python
import functools

import jax
import jax.numpy as jnp
from jax import lax
from jax.experimental import pallas as pl
from jax.experimental.pallas import tpu as pltpu

F32 = jnp.float32
BF16 = jnp.bfloat16
I32 = jnp.int32

EPS = 1e-6
LOG2_E = 1.4426950408889634
TOPK_MAX = 256
TOP_E = 2
LANES = 128
NEG_BIAS = -1e30
INT_MIN = -(2 ** 31)
KEY_NEG_INF = -2139095041
KEY_POS_INF = 2139095040
VMEM_LIMIT = 56 * 1024 * 1024
SLAB_UNROLL = 4


def _cparams(n_axes):
    return pltpu.CompilerParams(dimension_semantics=("arbitrary",) * n_axes,
                                vmem_limit_bytes=VMEM_LIMIT)


def _norm_mod(x, g, sh, sc):
    inv = lax.rsqrt(jnp.mean(x * x, axis=-1, keepdims=True) + EPS)
    return ((x * inv) * g) * (1.0 + sc) + sh


def _silu(x):
    return x * (1.0 / (1.0 + jnp.exp(-x)))


def _float_key(s):
    bits = lax.bitcast_convert_type(s, I32)
    return jnp.where(bits < 0, bits ^ jnp.int32(0x7FFFFFFF), bits)


def _adaln_kernel(c_ref, w_ref, b_ref, o_ref):
    a = _silu(c_ref[...]).astype(BF16)
    o_ref[0] = jnp.dot(a, w_ref[0].astype(BF16), preferred_element_type=F32) + b_ref[0]


def _adaln(c_all, w_ada, b_ada):
    depth, d, n = w_ada.shape
    r = c_all.shape[0]
    tn = n // 4
    return pl.pallas_call(
        _adaln_kernel,
        grid=(depth, n // tn),
        in_specs=[pl.BlockSpec((r, d), lambda l, j: (0, 0)),
                  pl.BlockSpec((1, d, tn), lambda l, j: (l, 0, j)),
                  pl.BlockSpec((1, 1, tn), lambda l, j: (l, 0, j))],
        out_specs=pl.BlockSpec((1, r, tn), lambda l, j: (l, 0, j)),
        out_shape=jax.ShapeDtypeStruct((depth, r, n), F32),
        compiler_params=_cparams(2),
        name="adaln",
    )(c_all, w_ada, b_ada.reshape(depth, 1, n))


def _mod_spec(tm, d, per_row):
    if per_row:
        return pl.BlockSpec((1, tm, d), lambda g, i, *_: (g, i, 0))
    return pl.BlockSpec((1, 1, d), lambda g, i, *_: (g, 0, 0))


def _attn_in_kernel(x_ref, sh_ref, sc_ref, g_ref, wq_ref, wk_ref, wv_ref, wqi_ref, wkw_ref,
                    q_ref, k_ref, v_ref, kb_ref, vb_ref, qi_ref, kw_ref, *, q_scale, qi_scale):
    h = _norm_mod(x_ref[0], g_ref[...], sh_ref[0], sc_ref[0]).astype(BF16)
    q_ref[0] = (jnp.dot(h, wq_ref[...], preferred_element_type=F32) * q_scale).astype(BF16)
    k = jnp.dot(h, wk_ref[...], preferred_element_type=F32)
    k_ref[0] = k
    kb_ref[0] = k.T.astype(BF16)
    v = jnp.dot(h, wv_ref[...], preferred_element_type=F32)
    v_ref[0] = v
    vb_ref[0] = v.astype(BF16)
    qi_ref[0] = (jnp.dot(h, wqi_ref[...], preferred_element_type=F32) * qi_scale).astype(BF16)
    kw_ref[0] = jnp.dot(h, wkw_ref[...], preferred_element_type=F32)


def _attn_in(x, sh, sc, g, ws, tm, per_row, q_scale, idx_dim):
    G, T, d = x.shape
    wq, wk, wv, wqi, wkw = ws
    nqi = wqi.shape[1]
    nkw = wkw.shape[1]
    full = lambda a: pl.BlockSpec(a.shape, lambda g_, i: (0,) * a.ndim)
    row = lambda n: pl.BlockSpec((1, tm, n), lambda g_, i: (g_, i, 0))
    kern = functools.partial(_attn_in_kernel, q_scale=q_scale, qi_scale=idx_dim ** -0.5)
    return pl.pallas_call(
        kern,
        grid=(G, T // tm),
        in_specs=[row(d), _mod_spec(tm, d, per_row), _mod_spec(tm, d, per_row), full(g),
                  full(wq), full(wk), full(wv), full(wqi), full(wkw)],
        out_specs=[row(d), row(d), row(d), pl.BlockSpec((1, d, tm), lambda g_, i: (g_, 0, i)),
                   row(d), row(nqi), row(nkw)],
        out_shape=[jax.ShapeDtypeStruct((G, T, d), BF16),
                   jax.ShapeDtypeStruct((G, T, d), F32),
                   jax.ShapeDtypeStruct((G, T, d), F32),
                   jax.ShapeDtypeStruct((G, d, T), BF16),
                   jax.ShapeDtypeStruct((G, T, d), BF16),
                   jax.ShapeDtypeStruct((G, T, nqi), BF16),
                   jax.ShapeDtypeStruct((G, T, nkw), F32)],
        compiler_params=_cparams(2),
        name="attn_in",
    )(x, sh, sc, g, wq, wk, wv, wqi, wkw)


def _count_groups(key_sc, n_slabs, groups, row_group, preds):
    def add(cnt, c, r0, pred):
        kk = key_sc[c, r0:r0 + row_group, :]
        return cnt + jnp.where(pred(kk, c), 1, 0).astype(I32)

    parts = []
    for r0, pred in zip(groups, preds):
        cnt = jnp.zeros((row_group, LANES), I32)
        if isinstance(n_slabs, int):
            for c in range(n_slabs):
                cnt = add(cnt, c, r0, pred)
        else:
            def body(cc, cnt, r0=r0, pred=pred):
                for u in range(SLAB_UNROLL):
                    cnt = add(cnt, cc * SLAB_UNROLL + u, r0, pred)
                return cnt
            cnt = lax.fori_loop(0, n_slabs // SLAB_UNROLL, body, cnt)
        parts.append(cnt)
    return [jnp.sum(p, axis=1, keepdims=True) for p in parts]


def _select_bias(key_sc, bias_sc, n_slabs, rows, row_group, topk):
    groups = list(range(0, rows, row_group))
    ng = len(groups)
    lane = lax.broadcasted_iota(I32, (row_group, LANES), 1)
    zeros = tuple(jnp.zeros((row_group, LANES), I32) for _ in groups)

    def bit_body(i, prefixes):
        bit = lax.shift_left(jnp.int32(1), 31 - i)
        trials = [p | bit for p in prefixes]
        preds = [lambda kk, c, ts=t ^ jnp.int32(INT_MIN): kk >= ts for t in trials]
        cnts = _count_groups(key_sc, n_slabs, groups, row_group, preds)
        return tuple(jnp.where(cnts[g] >= topk, trials[g], prefixes[g]) for g in range(ng))

    prefixes = lax.fori_loop(0, 32, bit_body, zeros)
    thrs = [p ^ jnp.int32(INT_MIN) for p in prefixes]
    n_gt = _count_groups(key_sc, n_slabs, groups, row_group,
                         [lambda kk, c, t=t: kk > t for t in thrs])
    n_eq = _count_groups(key_sc, n_slabs, groups, row_group,
                         [lambda kk, c, t=t: kk == t for t in thrs])
    needs = [topk - n for n in n_gt]
    tie = jnp.int32(0)
    for g in range(ng):
        tie = jnp.maximum(tie, jnp.max(jnp.where(n_eq[g] > needs[g], 1, 0)))

    def tie_cut():
        def cut_body(i, cuts):
            bit = lax.shift_left(jnp.int32(1), 30 - i)
            trials = [p | bit for p in cuts]
            preds = [lambda kk, c, t=thrs[g], tr=trials[g]: (kk == t) & (c * LANES + lane < tr)
                     for g in range(ng)]
            cnts = _count_groups(key_sc, n_slabs, groups, row_group, preds)
            return tuple(jnp.where(cnts[g] < needs[g], trials[g], cuts[g]) for g in range(ng))
        return lax.fori_loop(0, 31, cut_body, zeros)

    cuts = lax.cond(tie > 0, tie_cut,
                    lambda: tuple(jnp.full((row_group, LANES), 2 ** 31 - 1, I32) for _ in groups))

    def bias_body(c, carry):
        col = c * LANES + lane
        for g, r0 in enumerate(groups):
            kk = key_sc[c, r0:r0 + row_group, :]
            sel = (kk > thrs[g]) | ((kk == thrs[g]) & (col <= cuts[g]))
            fin = (kk > KEY_NEG_INF) & (kk < KEY_POS_INF)
            bias_sc[c, r0:r0 + row_group, :] = jnp.where(sel & fin, 0.0, NEG_BIAS).astype(F32)
        return carry
    lax.fori_loop(0, n_slabs, bias_body, 0)


def _prompt_attn_kernel(q_ref, qi_ref, w_ref, kT_ref, k_ref, v_ref, o_ref,
                        key_sc, bias_sc, m_sc, l_sc, acc_sc,
                        *, qb, tk, topk, n_heads, head_dim, n_ih, idx_dim, w_scale, row_group):
    qi_blk = pl.program_id(1)
    kj = pl.program_id(2)
    nk = pl.num_programs(2)
    n_valid = (qi_blk + 1) * qb
    n_chunks = (n_valid + tk - 1) // tk
    spc = tk // LANES

    @pl.when(kj == 0)
    def _indexer():
        qi = qi_ref[0]
        w = w_ref[0] * w_scale
        wcols = [w[:, h:h + 1] for h in range(n_ih)]
        row = qi_blk * qb + lax.broadcasted_iota(I32, (qb, tk), 0)
        lane = lax.broadcasted_iota(I32, (qb, tk), 1)

        def chunk_body(c, carry):
            kT = kT_ref[0, c]
            s = None
            for h in range(n_ih):
                d = jnp.dot(qi[:, h * idx_dim:(h + 1) * idx_dim], kT, preferred_element_type=F32)
                t = jnp.maximum(d, 0.0) * wcols[h]
                s = t if s is None else s + t
            key = jnp.where(c * tk + lane <= row, _float_key(s), jnp.int32(INT_MIN))
            for j in range(spc):
                key_sc[c * spc + j] = key[:, j * LANES:(j + 1) * LANES]
            return carry
        lax.fori_loop(0, n_chunks, chunk_body, 0)
        _select_bias(key_sc, bias_sc, n_chunks * spc, qb, row_group, topk)
        m_sc[...] = jnp.full(m_sc.shape, NEG_BIAS, F32)
        l_sc[...] = jnp.zeros(l_sc.shape, F32)
        acc_sc[...] = jnp.zeros(acc_sc.shape, F32)

    @pl.when(kj * tk < n_valid)
    def _attend():
        bias = jnp.concatenate([bias_sc[kj * spc + j] for j in range(spc)], axis=1)
        for h in range(n_heads):
            hs = slice(h * head_dim, (h + 1) * head_dim)
            s = jnp.dot(q_ref[0, :, hs], k_ref[0, hs, :], preferred_element_type=F32) + bias
            m_prev = m_sc[h]
            m_new = jnp.maximum(m_prev, jnp.max(s, axis=1, keepdims=True))
            alpha = jnp.exp2(m_prev - m_new)
            p = jnp.exp2(s - jnp.tile(m_new, (1, spc)))
            l_sc[h] = alpha * l_sc[h] + jnp.sum(p, axis=1, keepdims=True)
            pv = jnp.dot(p.astype(BF16), v_ref[0, :, hs], preferred_element_type=F32)
            acc_sc[h] = acc_sc[h] * alpha[:, :head_dim] + pv
            m_sc[h] = m_new

    @pl.when(kj == nk - 1)
    def _finish():
        outs = [acc_sc[h] * (1.0 / l_sc[h])[:, :head_dim] for h in range(n_heads)]
        o_ref[0] = jnp.concatenate(outs, axis=1).astype(o_ref.dtype)


def _prompt_attn(q, qi, w, kT, kb, vb, *, qb, tk, topk, n_heads, n_ih):
    B, T, d = q.shape
    head_dim = d // n_heads
    idx_dim = qi.shape[2] // n_ih
    nq, nk = T // qb, T // tk
    assert qb >= topk and tk % (LANES * SLAB_UNROLL) == 0 and qb <= tk
    kT = kT.reshape(B, idx_dim, nk, tk).transpose(0, 2, 1, 3)

    def last_tile(i, j):
        return jnp.minimum(j, ((i + 1) * qb - 1) // tk)

    kern = functools.partial(_prompt_attn_kernel, qb=qb, tk=tk, topk=topk, n_heads=n_heads,
                             head_dim=head_dim, n_ih=n_ih, idx_dim=idx_dim,
                             w_scale=n_ih ** -0.5, row_group=64)
    return pl.pallas_call(
        kern,
        grid=(B, nq, nk),
        in_specs=[pl.BlockSpec((1, qb, d), lambda b, i, j: (b, i, 0)),
                  pl.BlockSpec((1, qb, qi.shape[2]), lambda b, i, j: (b, i, 0)),
                  pl.BlockSpec((1, qb, w.shape[2]), lambda b, i, j: (b, i, 0)),
                  pl.BlockSpec((1, nk, idx_dim, tk), lambda b, i, j: (b, 0, 0, 0)),
                  pl.BlockSpec((1, d, tk), lambda b, i, j: (b, 0, last_tile(i, j))),
                  pl.BlockSpec((1, tk, d), lambda b, i, j: (b, last_tile(i, j), 0))],
        out_specs=pl.BlockSpec((1, qb, d), lambda b, i, j: (b, i, 0)),
        out_shape=jax.ShapeDtypeStruct((B, T, d), BF16),
        scratch_shapes=[pltpu.VMEM((T // LANES, qb, LANES), I32),
                        pltpu.VMEM((T // LANES, qb, LANES), F32),
                        pltpu.VMEM((n_heads, qb, LANES), F32),
                        pltpu.VMEM((n_heads, qb, LANES), F32),
                        pltpu.VMEM((n_heads, qb, head_dim), F32)],
        compiler_params=_cparams(3),
        name="prompt_attn",
    )(q, qi, w, kT, kb, vb)


def _sample_scores_kernel(pt_ref, qi_ref, w_ref, *rest, pg):
    o_ref = rest[pg]
    qi = qi_ref[0]
    w = w_ref[0]
    for p in range(pg):
        kp = rest[p][0, 0].astype(BF16)
        d = lax.dot_general(qi, kp, (((1,), (1,)), ((), ())), preferred_element_type=F32)
        o_ref[0, 0, p:p + 1, :] = jnp.sum(jnp.maximum(d, 0.0) * w, axis=0, keepdims=True)


def _sample_scores(page_table, qi3, w3, cache_kidx, layer, pg):
    DB, n_pages = page_table.shape
    _, _, page, idx_dim = cache_kidx.shape
    n_ih = qi3.shape[1]
    page_specs = [
        pl.BlockSpec((1, 1, page, idx_dim),
                     lambda b, j, pt, p=p: (layer, pt[b, j * pg + p], 0, 0))
        for p in range(pg)]
    grid_spec = pltpu.PrefetchScalarGridSpec(
        num_scalar_prefetch=1,
        grid=(DB, n_pages // pg),
        in_specs=[pl.BlockSpec((1, n_ih, idx_dim), lambda b, j, pt: (b, 0, 0)),
                  pl.BlockSpec((1, n_ih, 1), lambda b, j, pt: (b, 0, 0))] + page_specs,
        out_specs=pl.BlockSpec((1, 1, pg, page), lambda b, j, pt: (b, j, 0, 0)))
    out = pl.pallas_call(
        functools.partial(_sample_scores_kernel, pg=pg),
        grid_spec=grid_spec,
        out_shape=jax.ShapeDtypeStruct((DB, n_pages // pg, pg, page), F32),
        compiler_params=_cparams(2),
        name="sample_scores",
    )(page_table, qi3, w3, *([cache_kidx] * pg))
    return out.reshape(DB, n_pages * page)


def _sample_select_kernel(sp_ref, qi_ref, kn_ref, w_ref, bias_ref, key_sc, *, topk, n_ih, idx_dim,
                          w_scale, row_group):
    DB, past = sp_ref.shape
    n_past = past // LANES
    qi = qi_ref[...].astype(F32)
    kn = kn_ref[...]
    w = w_ref[...] * w_scale
    s_new = None
    for h in range(n_ih):
        d = jnp.sum(qi[:, h * idx_dim:(h + 1) * idx_dim] * kn, axis=1, keepdims=True)
        t = jnp.maximum(d, 0.0) * w[:, h:h + 1]
        s_new = t if s_new is None else s_new + t
    for c in range(n_past):
        key_sc[c] = _float_key(sp_ref[:, c * LANES:(c + 1) * LANES])
    lane = lax.broadcasted_iota(I32, (DB, LANES), 1)
    key_sc[n_past] = jnp.where(lane == 0, _float_key(jnp.broadcast_to(s_new, (DB, LANES))),
                               jnp.int32(INT_MIN))
    _select_bias(key_sc, bias_ref, n_past + 1, DB, row_group, topk)


def _sample_select(scores_past, qi, kn, w, topk, n_ih):
    DB, past = scores_past.shape
    idx_dim = kn.shape[1]
    n_slabs = past // LANES + 1
    kern = functools.partial(_sample_select_kernel, topk=topk, n_ih=n_ih, idx_dim=idx_dim,
                             w_scale=n_ih ** -0.5, row_group=min(64, DB))
    return pl.pallas_call(
        kern,
        out_shape=jax.ShapeDtypeStruct((n_slabs, DB, LANES), F32),
        scratch_shapes=[pltpu.VMEM((n_slabs, DB, LANES), I32)],
        compiler_params=pltpu.CompilerParams(vmem_limit_bytes=VMEM_LIMIT),
        name="sample_select",
    )(scores_past, qi, kn, w)


def _sample_attn_kernel(pt_ref, q_ref, kn_ref, vn_ref, bn_ref, bias_ref, *rest, pg, n_heads, head_dim):
    k_refs = rest[:pg]
    v_refs = rest[pg:2 * pg]
    o_ref = rest[2 * pg]
    m_sc, l_sc, acc_sc = rest[2 * pg + 1:]
    jg = pl.program_id(1)
    page = k_refs[0].shape[2]
    q = q_ref[0]

    @pl.when(jg == 0)
    def _init():
        s_new = jnp.sum(q * kn_ref[0], axis=1, keepdims=True) + bn_ref[0]
        m_sc[...] = s_new
        l_sc[...] = jnp.ones(l_sc.shape, F32)
        acc_sc[...] = vn_ref[0]

    for p in range(pg):
        bias = jnp.stack([jnp.full((1, 1), bias_ref[0, 0, p * page + i], F32) for i in range(page)])
        s = jnp.sum(k_refs[p][0, 0] * q[None], axis=2, keepdims=True) + bias
        m_prev = m_sc[...]
        m_new = jnp.maximum(m_prev, jnp.max(s, axis=0))
        alpha = jnp.exp(m_prev - m_new)
        pr = jnp.exp(s - m_new[None])
        l_sc[...] = alpha * l_sc[...] + jnp.sum(pr, axis=0)
        acc_sc[...] = alpha * acc_sc[...] + jnp.sum(pr * v_refs[p][0, 0], axis=0)
        m_sc[...] = m_new

    @pl.when(jg == pl.num_programs(1) - 1)
    def _finish():
        o_ref[0] = (acc_sc[...] * (1.0 / l_sc[...])).astype(o_ref.dtype)


def _sample_attn(page_table, q, kn, vn, bias, cache_k, cache_v, layer, pg):
    DB, n_pages = page_table.shape
    _, _, page, n_heads, head_dim = cache_k.shape
    n_slabs = bias.shape[0]
    past = n_pages * page
    assert n_slabs * LANES == past + LANES
    bias_rows = bias.transpose(1, 0, 2).reshape(DB, 1, n_slabs * LANES)
    bias_new = bias_rows[:, :, past:past + 1]
    blk = pg * page
    page_spec = lambda p: pl.BlockSpec(
        (1, 1, page, n_heads, head_dim), lambda b, j, pt, p=p: (layer, pt[b, j * pg + p], 0, 0, 0))
    head = pl.BlockSpec((1, n_heads, head_dim), lambda b, j, pt: (b, 0, 0))
    grid_spec = pltpu.PrefetchScalarGridSpec(
        num_scalar_prefetch=1,
        grid=(DB, n_pages // pg),
        in_specs=[head, head, head,
                  pl.BlockSpec((1, 1, 1), lambda b, j, pt: (b, 0, 0)),
                  pl.BlockSpec((1, 1, blk), lambda b, j, pt: (b, 0, j), memory_space=pltpu.SMEM)]
                 + [page_spec(p) for p in range(pg)] * 2,
        out_specs=head,
        scratch_shapes=[pltpu.VMEM((n_heads, 1), F32),
                        pltpu.VMEM((n_heads, 1), F32),
                        pltpu.VMEM((n_heads, head_dim), F32)])
    return pl.pallas_call(
        functools.partial(_sample_attn_kernel, pg=pg, n_heads=n_heads, head_dim=head_dim),
        grid_spec=grid_spec,
        out_shape=jax.ShapeDtypeStruct((DB, n_heads, head_dim), BF16),
        compiler_params=_cparams(2),
        name="sample_attn",
    )(page_table, q, kn, vn, bias_new, bias_rows, *([cache_k] * pg), *([cache_v] * pg))


def _proj_res_kernel(z_ref, w_ref, x_ref, gt_ref, o_ref):
    y = jnp.dot(z_ref[0], w_ref[...], preferred_element_type=F32)
    o_ref[0] = x_ref[0] + gt_ref[0] * y


def _proj_res(z, w, x, gt, tm, per_row):
    G, T, d = x.shape
    row = pl.BlockSpec((1, tm, d), lambda g, i: (g, i, 0))
    return pl.pallas_call(
        _proj_res_kernel,
        grid=(G, T // tm),
        in_specs=[row, pl.BlockSpec(w.shape, lambda g, i: (0, 0)), row, _mod_spec(tm, d, per_row)],
        out_specs=row,
        out_shape=jax.ShapeDtypeStruct((G, T, d), F32),
        compiler_params=_cparams(2),
        name="proj_res",
    )(z, w, x, gt)


def _conv_prompt_kernel(x_ref, sh_ref, sc_ref, gt_ref, g_ref, wb_ref, wc_ref, wx_ref, wcv_ref, wo_ref,
                        o_ref, st_ref, carry_sc, *, conv_w):
    i = pl.program_id(1)
    tm = x_ref.shape[1]
    x = x_ref[0]
    h = _norm_mod(x, g_ref[...], sh_ref[0], sc_ref[0]).astype(BF16)
    gb = jnp.dot(h, wb_ref[...], preferred_element_type=F32)
    u = (jnp.dot(h, wc_ref[...], preferred_element_type=F32)
         * jnp.dot(h, wx_ref[...], preferred_element_type=F32))

    @pl.when(i == 0)
    def _zero_state():
        carry_sc[...] = jnp.zeros(carry_sc.shape, F32)

    rows = lax.broadcasted_iota(I32, u.shape, 0)
    conv = u * wcv_ref[conv_w - 1:conv_w, :]
    for back in range(1, conv_w):
        shifted = pltpu.roll(u, back, axis=0)
        for r in range(back):
            prev = carry_sc[conv_w - 1 - back + r:conv_w - back + r, :]
            shifted = jnp.where(rows == r, prev, shifted)
        conv = conv + shifted * wcv_ref[conv_w - 1 - back:conv_w - back, :]
    carry_sc[...] = u[tm - (conv_w - 1):, :]
    st_ref[0] = u[tm - (conv_w - 1):, :]
    y = jnp.dot((gb * conv).astype(BF16), wo_ref[...], preferred_element_type=F32)
    o_ref[0] = x + gt_ref[0] * y


def _conv_prompt(x, sh, sc, gt, g, wb, wc, wx, wcv, wo, tm):
    B, T, d = x.shape
    conv_w = wcv.shape[0]
    row = pl.BlockSpec((1, tm, d), lambda b, i: (b, i, 0))
    full = lambda a: pl.BlockSpec(a.shape, lambda b, i: (0,) * a.ndim)
    mod = _mod_spec(tm, d, False)
    return pl.pallas_call(
        functools.partial(_conv_prompt_kernel, conv_w=conv_w),
        grid=(B, T // tm),
        in_specs=[row, mod, mod, mod, full(g), full(wb), full(wc), full(wx), full(wcv), full(wo)],
        out_specs=[row, pl.BlockSpec((1, conv_w - 1, d), lambda b, i: (b, 0, 0))],
        out_shape=[jax.ShapeDtypeStruct((B, T, d), F32),
                   jax.ShapeDtypeStruct((B, conv_w - 1, d), F32)],
        scratch_shapes=[pltpu.VMEM((conv_w - 1, d), F32)],
        compiler_params=_cparams(2),
        name="conv_prompt",
    )(x, sh, sc, gt, g, wb, wc, wx, wcv, wo)


def _conv_sample_kernel(x_ref, sh_ref, sc_ref, gt_ref, g_ref, st_ref, wb_ref, wc_ref, wx_ref, wcv_ref,
                        wo_ref, o_ref, ns_ref, *, conv_w):
    x = x_ref[...]
    h = _norm_mod(x, g_ref[...], sh_ref[...], sc_ref[...]).astype(BF16)
    gb = jnp.dot(h, wb_ref[...], preferred_element_type=F32)
    u = (jnp.dot(h, wc_ref[...], preferred_element_type=F32)
         * jnp.dot(h, wx_ref[...], preferred_element_type=F32))
    conv = u * wcv_ref[conv_w - 1:conv_w, :]
    for j in range(conv_w - 1):
        conv = conv + st_ref[j] * wcv_ref[j:j + 1, :]
    for j in range(conv_w - 2):
        ns_ref[j] = st_ref[j + 1]
    ns_ref[conv_w - 2] = u
    y = jnp.dot((gb * conv).astype(BF16), wo_ref[...], preferred_element_type=F32)
    o_ref[...] = x + gt_ref[...] * y


def _conv_sample(x, sh, sc, gt, g, state, wb, wc, wx, wcv, wo):
    DB, d = x.shape
    conv_w = wcv.shape[0]
    return pl.pallas_call(
        functools.partial(_conv_sample_kernel, conv_w=conv_w),
        out_shape=[jax.ShapeDtypeStruct((DB, d), F32),
                   jax.ShapeDtypeStruct((conv_w - 1, DB, d), F32)],
        compiler_params=pltpu.CompilerParams(vmem_limit_bytes=VMEM_LIMIT),
        name="conv_sample",
    )(x, sh, sc, gt, g, state, wb, wc, wx, wcv, wo)


def _norm_kernel(x_ref, sh_ref, sc_ref, g_ref, h_ref):
    h_ref[0] = _norm_mod(x_ref[0], g_ref[...], sh_ref[0], sc_ref[0]).astype(BF16)


def _norm_router_kernel(x_ref, sh_ref, sc_ref, g_ref, wr_ref, br_ref, h_ref, cb_ref, *, n_experts):
    h = _norm_mod(x_ref[0], g_ref[...], sh_ref[0], sc_ref[0])
    h_ref[0] = h.astype(BF16)
    logits = jnp.dot(h, wr_ref[...], preferred_element_type=F32,
                     precision=lax.Precision.HIGHEST) + br_ref[...]
    lane = lax.broadcasted_iota(I32, logits.shape, 1)
    ninf = jnp.float32(-jnp.inf)
    logits = jnp.where(lane < n_experts, logits, ninf)
    m1 = jnp.max(logits, axis=1, keepdims=True)
    i1 = jnp.min(jnp.where(logits == m1, lane, LANES), axis=1, keepdims=True)
    rest = jnp.where(lane == i1, ninf, logits)
    m2 = jnp.max(rest, axis=1, keepdims=True)
    i2 = jnp.min(jnp.where(rest == m2, lane, LANES), axis=1, keepdims=True)
    e2 = jnp.exp(m2 - m1)
    inv = 1.0 / (1.0 + e2)
    cb_ref[0] = jnp.where(lane == 0, i1.astype(F32), jnp.where(
        lane == 1, i2.astype(F32), jnp.where(lane == 2, inv, jnp.where(lane == 3, e2 * inv, 0.0))))


def _norm(x, sh, sc, g, tm, per_row, router=None):
    G, T, d = x.shape
    row = pl.BlockSpec((1, tm, d), lambda g_, i: (g_, i, 0))
    full = lambda a: pl.BlockSpec(a.shape, lambda g_, i: (0,) * a.ndim)
    mod = _mod_spec(tm, d, per_row)
    if router is None:
        return pl.pallas_call(
            _norm_kernel, grid=(G, T // tm),
            in_specs=[row, mod, mod, full(g)], out_specs=row,
            out_shape=jax.ShapeDtypeStruct((G, T, d), BF16),
            compiler_params=_cparams(2), name="norm",
        )(x, sh, sc, g)
    wr, br, n_experts = router
    return pl.pallas_call(
        functools.partial(_norm_router_kernel, n_experts=n_experts), grid=(G, T // tm),
        in_specs=[row, mod, mod, full(g), full(wr), full(br)],
        out_specs=[row, pl.BlockSpec((1, tm, LANES), lambda g_, i: (g_, i, 0))],
        out_shape=[jax.ShapeDtypeStruct((G, T, d), BF16),
                   jax.ShapeDtypeStruct((G, T, LANES), F32)],
        compiler_params=_cparams(2), name="norm_router",
    )(x, sh, sc, g, wr, br)


def _swiglu_step(h, wg_ref, wu_ref, wd_ref):
    gate = jnp.dot(h, wg_ref[0].astype(BF16), preferred_element_type=F32)
    up = jnp.dot(h, wu_ref[0].astype(BF16), preferred_element_type=F32)
    a = _silu(gate) * up
    return jnp.dot(a.astype(BF16), wd_ref[0].astype(BF16), preferred_element_type=F32)


def _ffn_kernel(h_ref, x_ref, gt_ref, wg_ref, wu_ref, wd_ref, o_ref, acc_sc):
    f = pl.program_id(2)

    @pl.when(f == 0)
    def _zero():
        acc_sc[...] = jnp.zeros(acc_sc.shape, F32)

    acc_sc[...] += _swiglu_step(h_ref[0], wg_ref, wu_ref, wd_ref)

    @pl.when(f == pl.num_programs(2) - 1)
    def _finish():
        o_ref[0] = x_ref[0] + gt_ref[0] * acc_sc[...]


def _ffn(h, x, gt, w_gu, w_down, e, tm, tf, per_row):
    G, T, d = x.shape
    d_ff = w_gu.shape[2] // 2
    nf = d_ff // tf
    row = pl.BlockSpec((1, tm, d), lambda g, i, f: (g, i, 0))
    return pl.pallas_call(
        _ffn_kernel,
        grid=(G, T // tm, nf),
        in_specs=[row, row, _mod_spec(tm, d, per_row),
                  pl.BlockSpec((1, d, tf), lambda g, i, f: (e, 0, f)),
                  pl.BlockSpec((1, d, tf), lambda g, i, f: (e, 0, nf + f)),
                  pl.BlockSpec((1, tf, d), lambda g, i, f: (e, f, 0))],
        out_specs=row,
        out_shape=jax.ShapeDtypeStruct((G, T, d), F32),
        scratch_shapes=[pltpu.VMEM((tm, d), F32)],
        compiler_params=_cparams(3),
        name="ffn",
    )(h, x, gt, w_gu, w_gu, w_down)


def _route_plan(route, n_experts, tm):
    n = route.shape[0]
    flat_e = route[:, :TOP_E].astype(I32).reshape(-1)
    flat_g = route[:, TOP_E:2 * TOP_E].reshape(-1)
    onehot = (flat_e[:, None] == jnp.arange(n_experts, dtype=I32)[None, :]).astype(I32)
    rank = jnp.sum((jnp.cumsum(onehot, axis=0) - onehot) * onehot, axis=1)
    counts = jnp.sum(onehot, axis=0)
    padded = (counts + tm - 1) // tm * tm
    ends = jnp.cumsum(padded)
    dest = (ends - padded)[flat_e] + rank
    n_tiles = (TOP_E * n + n_experts * (tm - 1) + tm - 1) // tm
    p_rows = n_tiles * tm
    src = jnp.zeros((p_rows,), I32).at[dest].set(jnp.arange(TOP_E * n, dtype=I32) // TOP_E)
    gate = jnp.zeros((p_rows,), F32).at[dest].set(flat_g)
    tile_start = jnp.arange(n_tiles, dtype=I32) * tm
    tile_expert = jnp.minimum(jnp.sum((ends[None, :] <= tile_start[:, None]).astype(I32), axis=1),
                              n_experts - 1)
    tile_valid = (tile_start < ends[-1]).astype(I32)
    return src, gate.reshape(p_rows, 1), dest.reshape(n, TOP_E), tile_expert, tile_valid


def _row_copy(src_hbm, idx, dst, r, sem):
    return pltpu.make_async_copy(src_hbm.at[pl.ds(idx, 1)], dst.at[pl.ds(r, 1)], sem)


def _gather_rows_kernel(idx_ref, src_hbm, o_ref, sem):
    rows = o_ref.shape[0]

    def start(r, c):
        _row_copy(src_hbm, idx_ref[0, 0, r], o_ref, r, sem).start()
        return c
    lax.fori_loop(0, rows, start, 0)

    def wait(r, c):
        _row_copy(src_hbm, idx_ref[0, 0, r], o_ref, r, sem).wait()
        return c
    lax.fori_loop(0, rows, wait, 0)


def _gather_rows(src, idx, rows):
    n, width = src.shape
    p = idx.shape[0]
    return pl.pallas_call(
        _gather_rows_kernel,
        grid=(p // rows,),
        in_specs=[pl.BlockSpec((1, 1, rows), lambda i: (i, 0, 0), memory_space=pltpu.SMEM),
                  pl.BlockSpec(memory_space=pl.ANY)],
        out_specs=pl.BlockSpec((rows, width), lambda i: (i, 0)),
        out_shape=jax.ShapeDtypeStruct((p, width), src.dtype),
        scratch_shapes=[pltpu.SemaphoreType.DMA(())],
        compiler_params=_cparams(1),
        name="gather_rows",
    )(idx.reshape(p // rows, 1, rows), src)


def _moe_ffn_kernel(te_ref, tv_ref, h_ref, gs_ref, wg_ref, wu_ref, wd_ref, o_ref, acc_sc):
    i = pl.program_id(0)
    f = pl.program_id(1)

    @pl.when(f == 0)
    def _zero():
        acc_sc[...] = jnp.zeros(acc_sc.shape, F32)

    @pl.when(tv_ref[i] > 0)
    def _compute():
        acc_sc[...] += _swiglu_step(h_ref[...], wg_ref, wu_ref, wd_ref)

    @pl.when(f == pl.num_programs(1) - 1)
    def _finish():
        o_ref[...] = acc_sc[...] * gs_ref[...]


def _moe_ffn(h_sorted, gate_sorted, tile_expert, tile_valid, w_gu, w_down, e_off, tm, tf):
    p, d = h_sorted.shape
    d_ff = w_gu.shape[2] // 2
    nf = d_ff // tf

    def w_map(col0):
        def index(i, f, te, tv):
            return (e_off + te[i], 0, col0 + jnp.where(tv[i] > 0, f, nf - 1))
        return index

    def wd_map(i, f, te, tv):
        return (e_off + te[i], jnp.where(tv[i] > 0, f, nf - 1), 0)

    grid_spec = pltpu.PrefetchScalarGridSpec(
        num_scalar_prefetch=2,
        grid=(p // tm, nf),
        in_specs=[pl.BlockSpec((tm, d), lambda i, f, te, tv: (i, 0)),
                  pl.BlockSpec((tm, 1), lambda i, f, te, tv: (i, 0)),
                  pl.BlockSpec((1, d, tf), w_map(0)),
                  pl.BlockSpec((1, d, tf), w_map(nf)),
                  pl.BlockSpec((1, tf, d), wd_map)],
        out_specs=pl.BlockSpec((tm, d), lambda i, f, te, tv: (i, 0)),
        scratch_shapes=[pltpu.VMEM((tm, d), F32)])
    return pl.pallas_call(
        _moe_ffn_kernel,
        grid_spec=grid_spec,
        out_shape=jax.ShapeDtypeStruct((p, d), F32),
        compiler_params=_cparams(2),
        name="moe_ffn",
    )(tile_expert, tile_valid, h_sorted, gate_sorted, w_gu, w_gu, w_down)


def _moe_combine_kernel(*refs):
    idx_refs = refs[:TOP_E]
    y_hbm, x_ref, gt_ref, o_ref = refs[TOP_E:TOP_E + 4]
    bufs = refs[TOP_E + 4:2 * TOP_E + 4]
    sem = refs[2 * TOP_E + 4]
    rows = x_ref.shape[1]

    def start(r, c):
        for j in range(TOP_E):
            _row_copy(y_hbm, idx_refs[j][0, 0, r], bufs[j], r, sem.at[j]).start()
        return c
    lax.fori_loop(0, rows, start, 0)

    def wait(r, c):
        for j in range(TOP_E):
            _row_copy(y_hbm, idx_refs[j][0, 0, r], bufs[j], r, sem.at[j]).wait()
        return c
    lax.fori_loop(0, rows, wait, 0)
    y = bufs[0][...]
    for j in range(1, TOP_E):
        y = y + bufs[j][...]
    o_ref[0] = x_ref[0] + gt_ref[0] * y


def _moe_combine(y_sorted, dest, x, gt, rows, per_row):
    G, T, d = x.shape
    nt = T // rows
    idx = [dest[:, j].reshape(G * nt, 1, rows) for j in range(TOP_E)]
    smem = pl.BlockSpec((1, 1, rows), lambda g, i: (g * nt + i, 0, 0), memory_space=pltpu.SMEM)
    row = pl.BlockSpec((1, rows, d), lambda g, i: (g, i, 0))
    return pl.pallas_call(
        _moe_combine_kernel,
        grid=(G, nt),
        in_specs=[smem] * TOP_E + [pl.BlockSpec(memory_space=pl.ANY), row, _mod_spec(rows, d, per_row)],
        out_specs=row,
        out_shape=jax.ShapeDtypeStruct((G, T, d), F32),
        scratch_shapes=[pltpu.VMEM((rows, d), F32) for _ in range(TOP_E)]
                       + [pltpu.SemaphoreType.DMA((TOP_E,))],
        compiler_params=_cparams(2),
        name="moe_combine",
    )(*idx, y_sorted, x, gt)


def _final_norm_kernel(x_ref, g_ref, o_ref):
    x = x_ref[0]
    inv = lax.rsqrt(jnp.mean(x * x, axis=-1, keepdims=True) + EPS)
    o_ref[0] = (x * inv) * g_ref[...]


def _final_norm(x, g, tm):
    G, T, d = x.shape
    row = pl.BlockSpec((1, tm, d), lambda g_, i: (g_, i, 0))
    return pl.pallas_call(
        _final_norm_kernel, grid=(G, T // tm),
        in_specs=[row, pl.BlockSpec(g.shape, lambda g_, i: (0, 0))], out_specs=row,
        out_shape=jax.ShapeDtypeStruct((G, T, d), F32),
        compiler_params=_cparams(2), name="final_norm",
    )(x, g)


def _largest_tile(n, cap):
    t = min(n, cap)
    while n % t:
        t //= 2
    return t


def kernel(x_prompt, x_sample, cache_k, cache_v, cache_kidx, state_conv, page_table, c_prompt, c_sample,
           w_ada, b_ada, g_norm_mix, g_norm_ffn, g_norm_final, w_attn_in, w_attn_out,
           w_conv_in, w_conv, w_conv_out, w_ffn_gate_up, w_ffn_down, w_router, b_router,
           w_moe_gate_up, w_moe_down):
    B, T, d = x_prompt.shape
    DB, Tn, _ = x_sample.shape
    assert Tn == 1
    depth = w_ada.shape[0]
    n_attn, n_pool, page, n_heads, head_dim = cache_k.shape
    idx_dim = cache_kidx.shape[-1]
    n_ih = (w_attn_in.shape[-1] - 3 * d - idx_dim) // (idx_dim + 1)
    n_pages = page_table.shape[1]
    past = n_pages * page
    n_experts = w_router.shape[-1]
    conv_w = w_conv.shape[1]
    topk_p = min(TOPK_MAX, T // 4)
    topk_s = min(TOPK_MAX, (past + Tn) // 4)

    tm_p = _largest_tile(T, 512)
    tm_f = _largest_tile(T, 1024)
    tf = _largest_tile(w_ffn_gate_up.shape[-1] // 2, 512)
    pg = _largest_tile(n_pages, 4)
    pg_i = _largest_tile(n_pages, 8)
    tk = _largest_tile(T, 512)
    qb = _largest_tile(T, 256)
    rows_g = _largest_tile(T, 256)

    r = B + DB
    r_pad = -(-r // 8) * 8
    c_all = jnp.concatenate([c_prompt, c_sample, jnp.zeros((r_pad - r, d), F32)], axis=0)
    mod = _adaln(c_all, w_ada, b_ada)

    w_moe_gu = w_moe_gate_up.reshape((-1,) + w_moe_gate_up.shape[2:])
    w_moe_dn = w_moe_down.reshape((-1,) + w_moe_down.shape[2:])
    xp = x_prompt
    xs = x_sample.reshape(1, DB, d)

    outs = {n: [] for n in ("kp", "vp", "ip", "cp", "ks", "vs", "is", "cs")}
    for l in range(depth):
        j = l // 2
        mp = [mod[l, :B, i * d:(i + 1) * d].reshape(B, 1, d) for i in range(6)]
        ms = [mod[l, B:r, i * d:(i + 1) * d].reshape(1, DB, d) for i in range(6)]
        g_mix = g_norm_mix[l].reshape(1, d)
        g_ffn = g_norm_ffn[l].reshape(1, d)
        if l % 2 == 0:
            wi = w_attn_in[j].astype(BF16)
            nkw = idx_dim + n_ih
            wkw = jnp.pad(wi[:, 3 * d + n_ih * idx_dim:], ((0, 0), (0, LANES - nkw)))
            ws = (wi[:, :d], wi[:, d:2 * d], wi[:, 2 * d:3 * d],
                  wi[:, 3 * d:3 * d + n_ih * idx_dim], wkw)
            wo = w_attn_out[j].astype(BF16)
            q, k, v, kbT, vb, qi, kw = _attn_in(xp, mp[0], mp[1], g_mix, ws, tm_p, False,
                                                head_dim ** -0.5 * LOG2_E, idx_dim)
            kidx = kw[:, :, :idx_dim]
            widx = kw[:, :, idx_dim:nkw]
            kT = jnp.swapaxes(kidx.astype(BF16), 1, 2)
            att = _prompt_attn(q, qi, widx, kT, kbT, vb, qb=qb, tk=tk, topk=topk_p,
                               n_heads=n_heads, n_ih=n_ih)
            xp = _proj_res(att, wo, xp, mp[2], tm_p, False)
            outs["kp"].append(k.reshape(B, T, n_heads, head_dim))
            outs["vp"].append(v.reshape(B, T, n_heads, head_dim))
            outs["ip"].append(kidx)
            q, k, v, _, _, qi, kw = _attn_in(xs, ms[0], ms[1], g_mix, ws, DB, True,
                                             head_dim ** -0.5, idx_dim)
            kidx = kw[0, :, :idx_dim]
            widx = kw[0, :, idx_dim:nkw]
            sc_past = _sample_scores(page_table, qi.reshape(DB, n_ih, idx_dim),
                                     (widx * n_ih ** -0.5).reshape(DB, n_ih, 1), cache_kidx, j, pg_i)
            bias = _sample_select(sc_past, qi[0], kidx, widx, topk_s, n_ih)
            heads = lambda a: a.astype(F32).reshape(DB, n_heads, head_dim)
            att = _sample_attn(page_table, heads(q), heads(k), heads(v), bias, cache_k, cache_v, j, pg)
            xs = _proj_res(att.reshape(1, DB, d), wo, xs, ms[2], DB, True)
            outs["ks"].append(k.reshape(DB, Tn, n_heads, head_dim))
            outs["vs"].append(v.reshape(DB, Tn, n_heads, head_dim))
            outs["is"].append(kidx.reshape(DB, Tn, idx_dim))
            hp = _norm(xp, mp[3], mp[4], g_ffn, tm_p, False)
            xp = _ffn(hp, xp, mp[5], w_ffn_gate_up, w_ffn_down, j, tm_f, tf, False)
            hs = _norm(xs, ms[3], ms[4], g_ffn, DB, True)
            xs = _ffn(hs, xs, ms[5], w_ffn_gate_up, w_ffn_down, j, DB, tf, True)
        else:
            wi = w_conv_in[j].astype(BF16)
            wb, wc, wx = wi[:, :d], wi[:, d:2 * d], wi[:, 2 * d:]
            wo = w_conv_out[j].astype(BF16)
            xp, st = _conv_prompt(xp, mp[0], mp[1], mp[2], g_mix, wb, wc, wx, w_conv[j], wo, tm_p)
            outs["cp"].append(st)
            xs2, ns = _conv_sample(xs[0], ms[0][0], ms[1][0], ms[2][0], g_mix,
                                   jnp.swapaxes(state_conv[j], 0, 1), wb, wc, wx, w_conv[j], wo)
            xs = xs2.reshape(1, DB, d)
            outs["cs"].append(jnp.swapaxes(ns, 0, 1))
            wr = jnp.pad(w_router[j], ((0, 0), (0, LANES - n_experts)))
            br = jnp.pad(b_router[j], (0, LANES - n_experts)).reshape(1, LANES)
            router = (wr, br, n_experts)
            hp, rp = _norm(xp, mp[3], mp[4], g_ffn, tm_p, False, router)
            hs, rs = _norm(xs, ms[3], ms[4], g_ffn, DB, True, router)
            h_all = jnp.concatenate([hp.reshape(B * T, d), hs.reshape(DB, d)], axis=0)
            route = jnp.concatenate([rp.reshape(B * T, LANES), rs.reshape(DB, LANES)], axis=0)
            src, gate, dest, t_exp, t_val = _route_plan(route, n_experts, tm_f)
            h_words = lax.bitcast_convert_type(h_all.reshape(B * T + DB, d // 2, 2), jnp.uint32)
            h_sorted = lax.bitcast_convert_type(_gather_rows(h_words, src, rows_g), BF16)
            y_sorted = _moe_ffn(h_sorted.reshape(-1, d), gate, t_exp, t_val, w_moe_gu, w_moe_dn,
                                j * n_experts, tm_f, tf)
            xp = _moe_combine(y_sorted, dest[:B * T], xp, mp[5], rows_g, False)
            xs = _moe_combine(y_sorted, dest[B * T:], xs, ms[5], _largest_tile(DB, rows_g), True)

    gf = g_norm_final.reshape(1, d)
    y_prompt = _final_norm(xp, gf, tm_p)
    y_sample = _final_norm(xs, gf, DB).reshape(DB, Tn, d)
    st = lambda n: jnp.stack(outs[n])
    return (y_prompt, y_sample, st("kp"), st("vp"), st("ip"), st("cp"),
            st("ks"), st("vs"), st("is"), st("cs"))
```

```python
import functools

import jax
import jax.numpy as jnp
from jax import lax
from jax.experimental import pallas as pl
from jax.experimental.pallas import tpu as pltpu

F32 = jnp.float32
BF16 = jnp.bfloat16
I32 = jnp.int32

EPS = 1e-6
LOG2_E = 1.4426950408889634
TOPK_MAX = 256
TOP_E = 2
LANES = 128
NEG_BIAS = -1e30
INT_MIN = -(2 ** 31)
KEY_NEG_INF = -2139095041
KEY_POS_INF = 2139095040
VMEM_LIMIT = 56 * 1024 * 1024
SLAB_UNROLL = 4


def _cparams(n_axes):
    return pltpu.CompilerParams(dimension_semantics=("arbitrary",) * n_axes,
                                vmem_limit_bytes=VMEM_LIMIT)


def _norm_mod(x, g, sh, sc):
    inv = lax.rsqrt(jnp.mean(x * x, axis=-1, keepdims=True) + EPS)
    return ((x * inv) * g) * (1.0 + sc) + sh


def _silu(x):
    return x * (1.0 / (1.0 + jnp.exp(-x)))


def _float_key(s):
    bits = lax.bitcast_convert_type(s, I32)
    return jnp.where(bits < 0, bits ^ jnp.int32(0x7FFFFFFF), bits)


def _adaln_kernel(c_ref, w_ref, b_ref, o_ref):
    a = _silu(c_ref[...]).astype(BF16)
    o_ref[0] = jnp.dot(a, w_ref[0].astype(BF16), preferred_element_type=F32) + b_ref[0]


def _adaln(c_all, w_ada, b_ada):
    depth, d, n = w_ada.shape
    r = c_all.shape[0]
    tn = n // 4
    return pl.pallas_call(
        _adaln_kernel,
        grid=(depth, n // tn),
        in_specs=[pl.BlockSpec((r, d), lambda l, j: (0, 0)),
                  pl.BlockSpec((1, d, tn), lambda l, j: (l, 0, j)),
                  pl.BlockSpec((1, 1, tn), lambda l, j: (l, 0, j))],
        out_specs=pl.BlockSpec((1, r, tn), lambda l, j: (l, 0, j)),
        out_shape=jax.ShapeDtypeStruct((depth, r, n), F32),
        compiler_params=_cparams(2),
        name="adaln",
    )(c_all, w_ada, b_ada.reshape(depth, 1, n))


def _mod_spec(tm, d, per_row):
    if per_row:
        return pl.BlockSpec((1, tm, d), lambda g, i, *_: (g, i, 0))
    return pl.BlockSpec((1, 1, d), lambda g, i, *_: (g, 0, 0))


def _dot_nt(a, b):
    return lax.dot_general(a, b, (((1,), (1,)), ((), ())), preferred_element_type=F32)


def _attn_in_kernel(x_ref, sh_ref, sc_ref, g_ref, wq_ref, wkt_ref, wv_ref, wvt_ref, wqi_ref, wkw_ref,
                    wkwt_ref, q_ref, kt_ref, vt_ref, kb_ref, vb_ref, qi_ref, kw_ref, kwt_ref,
                    *, q_scale, qi_scale):
    h = _norm_mod(x_ref[0], g_ref[...], sh_ref[0], sc_ref[0]).astype(BF16)
    q_ref[0] = (jnp.dot(h, wq_ref[...], preferred_element_type=F32) * q_scale).astype(BF16)
    kt = _dot_nt(wkt_ref[...], h)
    kt_ref[0] = kt
    kb_ref[0] = kt.astype(BF16)
    vt_ref[0] = _dot_nt(wvt_ref[...], h)
    vb_ref[0] = jnp.dot(h, wv_ref[...], preferred_element_type=F32).astype(BF16)
    qi_ref[0] = (jnp.dot(h, wqi_ref[...], preferred_element_type=F32) * qi_scale).astype(BF16)
    kw_ref[0] = jnp.dot(h, wkw_ref[...], preferred_element_type=F32)
    kwt_ref[0] = _dot_nt(wkwt_ref[...], h)


def _attn_in(x, sh, sc, g, ws, tm, per_row, q_scale, idx_dim):
    G, T, d = x.shape
    wq, wk, wv, wqi, wkw = ws
    ws = (wq, wk.T, wv, wv.T, wqi, wkw, wkw.T)
    nqi = wqi.shape[1]
    nkw = wkw.shape[1]
    full = lambda a: pl.BlockSpec(a.shape, lambda g_, i: (0,) * a.ndim)
    row = lambda n: pl.BlockSpec((1, tm, n), lambda g_, i: (g_, i, 0))
    col = lambda n: pl.BlockSpec((1, n, tm), lambda g_, i: (g_, 0, i))
    kern = functools.partial(_attn_in_kernel, q_scale=q_scale, qi_scale=idx_dim ** -0.5)
    return pl.pallas_call(
        kern,
        grid=(G, T // tm),
        in_specs=[row(d), _mod_spec(tm, d, per_row), _mod_spec(tm, d, per_row), full(g)]
                 + [full(w) for w in ws],
        out_specs=[row(d), col(d), col(d), col(d), row(d), row(nqi), row(nkw), col(nkw)],
        out_shape=[jax.ShapeDtypeStruct((G, T, d), BF16),
                   jax.ShapeDtypeStruct((G, d, T), F32),
                   jax.ShapeDtypeStruct((G, d, T), F32),
                   jax.ShapeDtypeStruct((G, d, T), BF16),
                   jax.ShapeDtypeStruct((G, T, d), BF16),
                   jax.ShapeDtypeStruct((G, T, nqi), BF16),
                   jax.ShapeDtypeStruct((G, T, nkw), F32),
                   jax.ShapeDtypeStruct((G, nkw, T), F32)],
        compiler_params=_cparams(2),
        name="attn_in",
    )(x, sh, sc, g, *ws)


def _count_groups(key_sc, n_slabs, groups, row_group, preds):
    def add(cnt, c, r0, pred):
        kk = key_sc[c, r0:r0 + row_group, :]
        return cnt + jnp.where(pred(kk, c), 1, 0).astype(I32)

    parts = []
    for r0, pred in zip(groups, preds):
        cnt = jnp.zeros((row_group, LANES), I32)
        if isinstance(n_slabs, int):
            for c in range(n_slabs):
                cnt = add(cnt, c, r0, pred)
        else:
            def body(cc, cnt, r0=r0, pred=pred):
                for u in range(SLAB_UNROLL):
                    cnt = add(cnt, cc * SLAB_UNROLL + u, r0, pred)
                return cnt
            cnt = lax.fori_loop(0, n_slabs // SLAB_UNROLL, body, cnt)
        parts.append(cnt)
    return [jnp.sum(p, axis=1, keepdims=True) for p in parts]


def _select_bias(key_sc, bias_sc, n_slabs, rows, row_group, topk):
    groups = list(range(0, rows, row_group))
    ng = len(groups)
    lane = lax.broadcasted_iota(I32, (row_group, LANES), 1)
    zeros = tuple(jnp.zeros((row_group, LANES), I32) for _ in groups)

    def bit_body(i, prefixes):
        bit = lax.shift_left(jnp.int32(1), 31 - i)
        trials = [p | bit for p in prefixes]
        preds = [lambda kk, c, ts=t ^ jnp.int32(INT_MIN): kk >= ts for t in trials]
        cnts = _count_groups(key_sc, n_slabs, groups, row_group, preds)
        return tuple(jnp.where(cnts[g] >= topk, trials[g], prefixes[g]) for g in range(ng))

    prefixes = lax.fori_loop(0, 32, bit_body, zeros)
    thrs = [p ^ jnp.int32(INT_MIN) for p in prefixes]
    n_gt = _count_groups(key_sc, n_slabs, groups, row_group,
                         [lambda kk, c, t=t: kk > t for t in thrs])
    n_eq = _count_groups(key_sc, n_slabs, groups, row_group,
                         [lambda kk, c, t=t: kk == t for t in thrs])
    needs = [topk - n for n in n_gt]
    tie = jnp.int32(0)
    for g in range(ng):
        tie = jnp.maximum(tie, jnp.max(jnp.where(n_eq[g] > needs[g], 1, 0)))

    def tie_cut():
        def cut_body(i, cuts):
            bit = lax.shift_left(jnp.int32(1), 30 - i)
            trials = [p | bit for p in cuts]
            preds = [lambda kk, c, t=thrs[g], tr=trials[g]: (kk == t) & (c * LANES + lane < tr)
                     for g in range(ng)]
            cnts = _count_groups(key_sc, n_slabs, groups, row_group, preds)
            return tuple(jnp.where(cnts[g] < needs[g], trials[g], cuts[g]) for g in range(ng))
        return lax.fori_loop(0, 31, cut_body, zeros)

    cuts = lax.cond(tie > 0, tie_cut,
                    lambda: tuple(jnp.full((row_group, LANES), 2 ** 31 - 1, I32) for _ in groups))

    def bias_body(c, carry):
        col = c * LANES + lane
        for g, r0 in enumerate(groups):
            kk = key_sc[c, r0:r0 + row_group, :]
            sel = (kk > thrs[g]) | ((kk == thrs[g]) & (col <= cuts[g]))
            fin = (kk > KEY_NEG_INF) & (kk < KEY_POS_INF)
            bias_sc[c, r0:r0 + row_group, :] = jnp.where(sel & fin, 0.0, NEG_BIAS).astype(F32)
        return carry
    lax.fori_loop(0, n_slabs, bias_body, 0)


def _prompt_attn_kernel(q_ref, qi_ref, w_ref, kT_ref, k_ref, v_ref, o_ref,
                        key_sc, bias_sc, m_sc, l_sc, acc_sc,
                        *, qb, tk, topk, n_heads, head_dim, n_ih, idx_dim, w_scale, row_group):
    qi_blk = pl.program_id(1)
    kj = pl.program_id(2)
    nk = pl.num_programs(2)
    n_valid = (qi_blk + 1) * qb
    n_chunks = (n_valid + tk - 1) // tk
    spc = tk // LANES

    @pl.when(kj == 0)
    def _indexer():
        qi = qi_ref[0]
        w = w_ref[0] * w_scale
        wcols = [w[:, h:h + 1] for h in range(n_ih)]
        row = qi_blk * qb + lax.broadcasted_iota(I32, (qb, tk), 0)
        lane = lax.broadcasted_iota(I32, (qb, tk), 1)

        def chunk_body(c, carry):
            kT = kT_ref[0, c]
            s = None
            for h in range(n_ih):
                d = jnp.dot(qi[:, h * idx_dim:(h + 1) * idx_dim], kT, preferred_element_type=F32)
                t = jnp.maximum(d, 0.0) * wcols[h]
                s = t if s is None else s + t
            key = jnp.where(c * tk + lane <= row, _float_key(s), jnp.int32(INT_MIN))
            for j in range(spc):
                key_sc[c * spc + j] = key[:, j * LANES:(j + 1) * LANES]
            return carry
        lax.fori_loop(0, n_chunks, chunk_body, 0)
        _select_bias(key_sc, bias_sc, n_chunks * spc, qb, row_group, topk)
        m_sc[...] = jnp.full(m_sc.shape, NEG_BIAS, F32)
        l_sc[...] = jnp.zeros(l_sc.shape, F32)
        acc_sc[...] = jnp.zeros(acc_sc.shape, F32)

    @pl.when(kj * tk < n_valid)
    def _attend():
        bias = jnp.concatenate([bias_sc[kj * spc + j] for j in range(spc)], axis=1)
        for h in range(n_heads):
            hs = slice(h * head_dim, (h + 1) * head_dim)
            s = jnp.dot(q_ref[0, :, hs], k_ref[0, hs, :], preferred_element_type=F32) + bias
            m_prev = m_sc[h]
            m_new = jnp.maximum(m_prev, jnp.max(s, axis=1, keepdims=True))
            alpha = jnp.exp2(m_prev - m_new)
            p = jnp.exp2(s - jnp.tile(m_new, (1, spc)))
            l_sc[h] = alpha * l_sc[h] + jnp.sum(p, axis=1, keepdims=True)
            pv = jnp.dot(p.astype(BF16), v_ref[0, :, hs], preferred_element_type=F32)
            acc_sc[h] = acc_sc[h] * alpha[:, :head_dim] + pv
            m_sc[h] = m_new

    @pl.when(kj == nk - 1)
    def _finish():
        outs = [acc_sc[h] * (1.0 / l_sc[h])[:, :head_dim] for h in range(n_heads)]
        o_ref[0] = jnp.concatenate(outs, axis=1).astype(o_ref.dtype)


def _prompt_attn(q, qi, w, kT, kb, vb, *, qb, tk, topk, n_heads, n_ih):
    B, T, d = q.shape
    head_dim = d // n_heads
    idx_dim = qi.shape[2] // n_ih
    nq, nk = T // qb, T // tk
    assert qb >= topk and tk % (LANES * SLAB_UNROLL) == 0 and qb <= tk
    kT = kT.reshape(B, idx_dim, nk, tk).transpose(0, 2, 1, 3)

    def last_tile(i, j):
        return jnp.minimum(j, ((i + 1) * qb - 1) // tk)

    kern = functools.partial(_prompt_attn_kernel, qb=qb, tk=tk, topk=topk, n_heads=n_heads,
                             head_dim=head_dim, n_ih=n_ih, idx_dim=idx_dim,
                             w_scale=n_ih ** -0.5, row_group=64)
    return pl.pallas_call(
        kern,
        grid=(B, nq, nk),
        in_specs=[pl.BlockSpec((1, qb, d), lambda b, i, j: (b, i, 0)),
                  pl.BlockSpec((1, qb, qi.shape[2]), lambda b, i, j: (b, i, 0)),
                  pl.BlockSpec((1, qb, w.shape[2]), lambda b, i, j: (b, i, 0)),
                  pl.BlockSpec((1, nk, idx_dim, tk), lambda b, i, j: (b, 0, 0, 0)),
                  pl.BlockSpec((1, d, tk), lambda b, i, j: (b, 0, last_tile(i, j))),
                  pl.BlockSpec((1, tk, d), lambda b, i, j: (b, last_tile(i, j), 0))],
        out_specs=pl.BlockSpec((1, qb, d), lambda b, i, j: (b, i, 0)),
        out_shape=jax.ShapeDtypeStruct((B, T, d), BF16),
        scratch_shapes=[pltpu.VMEM((T // LANES, qb, LANES), I32),
                        pltpu.VMEM((T // LANES, qb, LANES), F32),
                        pltpu.VMEM((n_heads, qb, LANES), F32),
                        pltpu.VMEM((n_heads, qb, LANES), F32),
                        pltpu.VMEM((n_heads, qb, head_dim), F32)],
        compiler_params=_cparams(3),
        name="prompt_attn",
    )(q, qi, w, kT, kb, vb)


def _sample_scores_kernel(pt_ref, qi_ref, w_ref, *rest, pg):
    o_ref = rest[pg]
    qi = qi_ref[0]
    w = w_ref[0]
    for p in range(pg):
        kp = rest[p][0, 0].astype(BF16).astype(F32)
        d = jnp.dot(qi.astype(F32), kp, preferred_element_type=F32)
        o_ref[0, 0, p:p + 1, :] = jnp.sum(jnp.maximum(d, 0.0) * w, axis=0, keepdims=True)


def _sample_scores(page_table, qi3, w3, cache_kidx_t, layer, pg):
    DB, n_pages = page_table.shape
    _, _, idx_dim, page = cache_kidx_t.shape
    n_ih = qi3.shape[1]
    page_specs = [
        pl.BlockSpec((1, 1, idx_dim, page),
                     lambda b, j, pt, p=p: (layer, pt[b, j * pg + p], 0, 0))
        for p in range(pg)]
    grid_spec = pltpu.PrefetchScalarGridSpec(
        num_scalar_prefetch=1,
        grid=(DB, n_pages // pg),
        in_specs=[pl.BlockSpec((1, n_ih, idx_dim), lambda b, j, pt: (b, 0, 0)),
                  pl.BlockSpec((1, n_ih, 1), lambda b, j, pt: (b, 0, 0))] + page_specs,
        out_specs=pl.BlockSpec((1, 1, pg, page), lambda b, j, pt: (b, j, 0, 0)))
    out = pl.pallas_call(
        functools.partial(_sample_scores_kernel, pg=pg),
        grid_spec=grid_spec,
        out_shape=jax.ShapeDtypeStruct((DB, n_pages // pg, pg, page), F32),
        compiler_params=_cparams(2),
        name="sample_scores",
    )(page_table, qi3, w3, *([cache_kidx_t] * pg))
    return out.reshape(DB, n_pages * page)


def _sample_select_kernel(sp_ref, qi_ref, kn_ref, w_ref, bias_ref, key_sc, *, topk, n_ih, idx_dim,
                          w_scale, row_group):
    DB, past = sp_ref.shape
    n_past = past // LANES
    qi = qi_ref[...].astype(F32)
    kn = kn_ref[...]
    w = w_ref[...] * w_scale
    s_new = None
    for h in range(n_ih):
        d = jnp.sum(qi[:, h * idx_dim:(h + 1) * idx_dim] * kn, axis=1, keepdims=True)
        t = jnp.maximum(d, 0.0) * w[:, h:h + 1]
        s_new = t if s_new is None else s_new + t
    for c in range(n_past):
        key_sc[c] = _float_key(sp_ref[:, c * LANES:(c + 1) * LANES])
    lane = lax.broadcasted_iota(I32, (DB, LANES), 1)
    key_sc[n_past] = jnp.where(lane == 0, _float_key(jnp.broadcast_to(s_new, (DB, LANES))),
                               jnp.int32(INT_MIN))
    _select_bias(key_sc, bias_ref, n_past + 1, DB, row_group, topk)


def _sample_select(scores_past, qi, kn, w, topk, n_ih):
    DB, past = scores_past.shape
    idx_dim = kn.shape[1]
    n_slabs = past // LANES + 1
    kern = functools.partial(_sample_select_kernel, topk=topk, n_ih=n_ih, idx_dim=idx_dim,
                             w_scale=n_ih ** -0.5, row_group=min(64, DB))
    return pl.pallas_call(
        kern,
        out_shape=jax.ShapeDtypeStruct((n_slabs, DB, LANES), F32),
        scratch_shapes=[pltpu.VMEM((n_slabs, DB, LANES), I32)],
        compiler_params=pltpu.CompilerParams(vmem_limit_bytes=VMEM_LIMIT),
        name="sample_select",
    )(scores_past, qi, kn, w)


def _sample_attn_kernel(pt_ref, q_ref, kn_ref, vn_ref, bn_ref, bias_ref, *rest, pg, n_heads, head_dim):
    k_refs = rest[:pg]
    v_refs = rest[pg:2 * pg]
    o_ref = rest[2 * pg]
    m_sc, l_sc, acc_sc = rest[2 * pg + 1:]
    jg = pl.program_id(1)
    page = k_refs[0].shape[4]
    q = q_ref[0]

    def update(k_tiles, v_tiles, bias):
        s = jnp.concatenate([jnp.sum(kt * q, axis=1) for kt in k_tiles], axis=1) + bias
        m_prev = m_sc[...]
        m_new = jnp.maximum(m_prev, jnp.max(s, axis=1, keepdims=True))
        alpha = jnp.exp(m_prev - m_new)
        pr = jnp.exp(s - m_new)
        l_sc[...] = alpha * l_sc[...] + jnp.sum(pr, axis=1, keepdims=True)
        m_sc[...] = m_new
        for h in range(n_heads):
            acc = acc_sc[h] * alpha[h:h + 1, :]
            for i, vt in enumerate(v_tiles):
                acc = acc + vt[h] * pr[h:h + 1, i * LANES:(i + 1) * LANES]
            acc_sc[h] = acc

    @pl.when(jg == 0)
    def _init():
        m_sc[...] = jnp.full(m_sc.shape, NEG_BIAS, F32)
        l_sc[...] = jnp.zeros(l_sc.shape, F32)
        acc_sc[...] = jnp.zeros(acc_sc.shape, F32)
        update([kn_ref[0]], [vn_ref[0]], bn_ref[0])

    for p0 in range(0, pg, 2):
        tiles = range(p0, min(p0 + 2, pg))
        update([k_refs[p][0, 0] for p in tiles], [v_refs[p][0, 0] for p in tiles],
               bias_ref[0][:, p0 * page:(p0 + len(tiles)) * page])

    @pl.when(jg == pl.num_programs(1) - 1)
    def _finish():
        inv = 1.0 / l_sc[...]
        for h in range(n_heads):
            o_ref[0, h] = jnp.sum(acc_sc[h], axis=1, keepdims=True) * inv[h:h + 1, :]


def _sample_attn(page_table, q, kn, vn, bias, cache_k_t, cache_v_t, layer, pg):
    DB, n_pages = page_table.shape
    _, _, n_heads, head_dim, page = cache_k_t.shape
    n_slabs = bias.shape[0]
    past = n_pages * page
    assert page == LANES and n_slabs * LANES == past + LANES
    bias_rows = bias.transpose(1, 0, 2).reshape(DB, 1, n_slabs * LANES)
    lanes = lambda a: jnp.broadcast_to(a[..., None], a.shape + (LANES,))
    blk = pg * page
    page_spec = lambda p: pl.BlockSpec(
        (1, 1, n_heads, head_dim, page), lambda b, j, pt, p=p: (layer, pt[b, j * pg + p], 0, 0, 0))
    head = pl.BlockSpec((1, n_heads, head_dim, LANES), lambda b, j, pt: (b, 0, 0, 0))
    grid_spec = pltpu.PrefetchScalarGridSpec(
        num_scalar_prefetch=1,
        grid=(DB, n_pages // pg),
        in_specs=[head, head, head,
                  pl.BlockSpec((1, 1, LANES), lambda b, j, pt: (b, 0, past // LANES)),
                  pl.BlockSpec((1, 1, blk), lambda b, j, pt: (b, 0, j))]
                 + [page_spec(p) for p in range(pg)] * 2,
        out_specs=pl.BlockSpec((1, n_heads, head_dim, 1), lambda b, j, pt: (b, 0, 0, 0)),
        scratch_shapes=[pltpu.VMEM((n_heads, 1), F32),
                        pltpu.VMEM((n_heads, 1), F32),
                        pltpu.VMEM((n_heads, head_dim, LANES), F32)])
    out = pl.pallas_call(
        functools.partial(_sample_attn_kernel, pg=pg, n_heads=n_heads, head_dim=head_dim),
        grid_spec=grid_spec,
        out_shape=jax.ShapeDtypeStruct((DB, n_heads, head_dim, 1), F32),
        compiler_params=_cparams(2),
        name="sample_attn",
    )(page_table, lanes(q), lanes(kn), lanes(vn), bias_rows, bias_rows,
      *([cache_k_t] * pg), *([cache_v_t] * pg))
    return out.reshape(DB, n_heads * head_dim)


def _proj_res_kernel(z_ref, w_ref, x_ref, gt_ref, o_ref):
    y = jnp.dot(z_ref[0], w_ref[...], preferred_element_type=F32)
    o_ref[0] = x_ref[0] + gt_ref[0] * y


def _proj_res(z, w, x, gt, tm, per_row):
    G, T, d = x.shape
    row = pl.BlockSpec((1, tm, d), lambda g, i: (g, i, 0))
    return pl.pallas_call(
        _proj_res_kernel,
        grid=(G, T // tm),
        in_specs=[row, pl.BlockSpec(w.shape, lambda g, i: (0, 0)), row, _mod_spec(tm, d, per_row)],
        out_specs=row,
        out_shape=jax.ShapeDtypeStruct((G, T, d), F32),
        compiler_params=_cparams(2),
        name="proj_res",
    )(z, w, x, gt)


def _conv_prompt_kernel(x_ref, sh_ref, sc_ref, gt_ref, g_ref, wb_ref, wc_ref, wx_ref, wcv_ref, wo_ref,
                        o_ref, st_ref, carry_sc, *, conv_w):
    i = pl.program_id(1)
    tm = x_ref.shape[1]
    x = x_ref[0]
    h = _norm_mod(x, g_ref[...], sh_ref[0], sc_ref[0]).astype(BF16)
    gb = jnp.dot(h, wb_ref[...], preferred_element_type=F32)
    u = (jnp.dot(h, wc_ref[...], preferred_element_type=F32)
         * jnp.dot(h, wx_ref[...], preferred_element_type=F32))

    @pl.when(i == 0)
    def _zero_state():
        carry_sc[...] = jnp.zeros(carry_sc.shape, F32)

    rows = lax.broadcasted_iota(I32, u.shape, 0)
    conv = u * wcv_ref[conv_w - 1:conv_w, :]
    for back in range(1, conv_w):
        shifted = pltpu.roll(u, back, axis=0)
        for r in range(back):
            prev = carry_sc[conv_w - 1 - back + r:conv_w - back + r, :]
            shifted = jnp.where(rows == r, prev, shifted)
        conv = conv + shifted * wcv_ref[conv_w - 1 - back:conv_w - back, :]
    carry_sc[...] = u[tm - (conv_w - 1):, :]
    st_ref[0] = u[tm - (conv_w - 1):, :]
    y = jnp.dot((gb * conv).astype(BF16), wo_ref[...], preferred_element_type=F32)
    o_ref[0] = x + gt_ref[0] * y


def _conv_prompt(x, sh, sc, gt, g, wb, wc, wx, wcv, wo, tm):
    B, T, d = x.shape
    conv_w = wcv.shape[0]
    row = pl.BlockSpec((1, tm, d), lambda b, i: (b, i, 0))
    full = lambda a: pl.BlockSpec(a.shape, lambda b, i: (0,) * a.ndim)
    mod = _mod_spec(tm, d, False)
    return pl.pallas_call(
        functools.partial(_conv_prompt_kernel, conv_w=conv_w),
        grid=(B, T // tm),
        in_specs=[row, mod, mod, mod, full(g), full(wb), full(wc), full(wx), full(wcv), full(wo)],
        out_specs=[row, pl.BlockSpec((1, conv_w - 1, d), lambda b, i: (b, 0, 0))],
        out_shape=[jax.ShapeDtypeStruct((B, T, d), F32),
                   jax.ShapeDtypeStruct((B, conv_w - 1, d), F32)],
        scratch_shapes=[pltpu.VMEM((conv_w - 1, d), F32)],
        compiler_params=_cparams(2),
        name="conv_prompt",
    )(x, sh, sc, gt, g, wb, wc, wx, wcv, wo)


def _conv_sample_kernel(x_ref, sh_ref, sc_ref, gt_ref, g_ref, st_ref, wb_ref, wc_ref, wx_ref, wcv_ref,
                        wo_ref, o_ref, ns_ref, *, conv_w):
    x = x_ref[...]
    h = _norm_mod(x, g_ref[...], sh_ref[...], sc_ref[...]).astype(BF16)
    gb = jnp.dot(h, wb_ref[...], preferred_element_type=F32)
    u = (jnp.dot(h, wc_ref[...], preferred_element_type=F32)
         * jnp.dot(h, wx_ref[...], preferred_element_type=F32))
    conv = u * wcv_ref[conv_w - 1:conv_w, :]
    for j in range(conv_w - 1):
        conv = conv + st_ref[j] * wcv_ref[j:j + 1, :]
    for j in range(conv_w - 2):
        ns_ref[j] = st_ref[j + 1]
    ns_ref[conv_w - 2] = u
    y = jnp.dot((gb * conv).astype(BF16), wo_ref[...], preferred_element_type=F32)
    o_ref[...] = x + gt_ref[...] * y


def _conv_sample(x, sh, sc, gt, g, state, wb, wc, wx, wcv, wo):
    DB, d = x.shape
    conv_w = wcv.shape[0]
    return pl.pallas_call(
        functools.partial(_conv_sample_kernel, conv_w=conv_w),
        out_shape=[jax.ShapeDtypeStruct((DB, d), F32),
                   jax.ShapeDtypeStruct((conv_w - 1, DB, d), F32)],
        compiler_params=pltpu.CompilerParams(vmem_limit_bytes=VMEM_LIMIT),
        name="conv_sample",
    )(x, sh, sc, gt, g, state, wb, wc, wx, wcv, wo)


def _norm_kernel(x_ref, sh_ref, sc_ref, g_ref, h_ref):
    h_ref[0] = _norm_mod(x_ref[0], g_ref[...], sh_ref[0], sc_ref[0]).astype(BF16)


def _norm_router_kernel(x_ref, sh_ref, sc_ref, g_ref, wr_ref, br_ref, h_ref, cb_ref, *, n_experts):
    h = _norm_mod(x_ref[0], g_ref[...], sh_ref[0], sc_ref[0])
    h_ref[0] = h
    logits = jnp.dot(h, wr_ref[...], preferred_element_type=F32,
                     precision=lax.Precision.HIGHEST) + br_ref[...]
    lane = lax.broadcasted_iota(I32, logits.shape, 1)
    ninf = jnp.float32(-jnp.inf)
    logits = jnp.where(lane < n_experts, logits, ninf)
    m1 = jnp.max(logits, axis=1, keepdims=True)
    i1 = jnp.min(jnp.where(logits == m1, lane, LANES), axis=1, keepdims=True)
    rest = jnp.where(lane == i1, ninf, logits)
    m2 = jnp.max(rest, axis=1, keepdims=True)
    i2 = jnp.min(jnp.where(rest == m2, lane, LANES), axis=1, keepdims=True)
    e2 = jnp.exp(m2 - m1)
    inv = 1.0 / (1.0 + e2)
    cb_ref[0] = jnp.where(lane == 0, i1.astype(F32), jnp.where(
        lane == 1, i2.astype(F32), jnp.where(lane == 2, inv, jnp.where(lane == 3, e2 * inv, 0.0))))


def _norm(x, sh, sc, g, tm, per_row, router=None):
    G, T, d = x.shape
    row = pl.BlockSpec((1, tm, d), lambda g_, i: (g_, i, 0))
    full = lambda a: pl.BlockSpec(a.shape, lambda g_, i: (0,) * a.ndim)
    mod = _mod_spec(tm, d, per_row)
    if router is None:
        return pl.pallas_call(
            _norm_kernel, grid=(G, T // tm),
            in_specs=[row, mod, mod, full(g)], out_specs=row,
            out_shape=jax.ShapeDtypeStruct((G, T, d), BF16),
            compiler_params=_cparams(2), name="norm",
        )(x, sh, sc, g)
    wr, br, n_experts = router
    return pl.pallas_call(
        functools.partial(_norm_router_kernel, n_experts=n_experts), grid=(G, T // tm),
        in_specs=[row, mod, mod, full(g), full(wr), full(br)],
        out_specs=[row, pl.BlockSpec((1, tm, LANES), lambda g_, i: (g_, i, 0))],
        out_shape=[jax.ShapeDtypeStruct((G, T, d), F32),
                   jax.ShapeDtypeStruct((G, T, LANES), F32)],
        compiler_params=_cparams(2), name="norm_router",
    )(x, sh, sc, g, wr, br)


def _swiglu_step(h, wg_ref, wu_ref, wd_ref):
    gate = jnp.dot(h, wg_ref[0].astype(BF16), preferred_element_type=F32)
    up = jnp.dot(h, wu_ref[0].astype(BF16), preferred_element_type=F32)
    a = _silu(gate) * up
    return jnp.dot(a.astype(BF16), wd_ref[0].astype(BF16), preferred_element_type=F32)


def _ffn_kernel(h_ref, x_ref, gt_ref, wg_ref, wu_ref, wd_ref, o_ref, acc_sc):
    f = pl.program_id(2)

    @pl.when(f == 0)
    def _zero():
        acc_sc[...] = jnp.zeros(acc_sc.shape, F32)

    acc_sc[...] += _swiglu_step(h_ref[0], wg_ref, wu_ref, wd_ref)

    @pl.when(f == pl.num_programs(2) - 1)
    def _finish():
        o_ref[0] = x_ref[0] + gt_ref[0] * acc_sc[...]


def _ffn(h, x, gt, w_gu, w_down, e, tm, tf, per_row):
    G, T, d = x.shape
    d_ff = w_gu.shape[2] // 2
    nf = d_ff // tf
    row = pl.BlockSpec((1, tm, d), lambda g, i, f: (g, i, 0))
    return pl.pallas_call(
        _ffn_kernel,
        grid=(G, T // tm, nf),
        in_specs=[row, row, _mod_spec(tm, d, per_row),
                  pl.BlockSpec((1, d, tf), lambda g, i, f: (e, 0, f)),
                  pl.BlockSpec((1, d, tf), lambda g, i, f: (e, 0, nf + f)),
                  pl.BlockSpec((1, tf, d), lambda g, i, f: (e, f, 0))],
        out_specs=row,
        out_shape=jax.ShapeDtypeStruct((G, T, d), F32),
        scratch_shapes=[pltpu.VMEM((tm, d), F32)],
        compiler_params=_cparams(3),
        name="ffn",
    )(h, x, gt, w_gu, w_gu, w_down)


def _route_plan(route, n_experts, tm):
    n = route.shape[0]
    flat_e = route[:, :TOP_E].astype(I32).reshape(-1)
    onehot = (flat_e[:, None] == jnp.arange(n_experts, dtype=I32)[None, :]).astype(I32)
    rank = jnp.sum((jnp.cumsum(onehot, axis=0) - onehot) * onehot, axis=1)
    counts = jnp.sum(onehot, axis=0)
    padded = (counts + tm - 1) // tm * tm
    ends = jnp.cumsum(padded)
    dest = (ends - padded)[flat_e] + rank
    n_tiles = (TOP_E * n + n_experts * (tm - 1) + tm - 1) // tm
    p_rows = n_tiles * tm
    src = jnp.zeros((p_rows,), I32).at[dest].set(jnp.arange(TOP_E * n, dtype=I32) // TOP_E)
    tile_start = jnp.arange(n_tiles, dtype=I32) * tm
    tile_expert = jnp.minimum(jnp.sum((ends[None, :] <= tile_start[:, None]).astype(I32), axis=1),
                              n_experts - 1)
    tile_valid = (tile_start < ends[-1]).astype(I32)
    return src, dest.reshape(n, TOP_E), tile_expert, tile_valid


def _row_copy(src_hbm, idx, dst, r, sem):
    return pltpu.make_async_copy(src_hbm.at[pl.ds(idx, 1)], dst.at[pl.ds(r, 1)], sem)


def _gather_rows_kernel(idx_ref, src_hbm, o_ref, sem):
    rows = o_ref.shape[0]

    def start(r, c):
        _row_copy(src_hbm, idx_ref[0, 0, r], o_ref, r, sem).start()
        return c
    lax.fori_loop(0, rows, start, 0)

    def wait(r, c):
        _row_copy(src_hbm, idx_ref[0, 0, r], o_ref, r, sem).wait()
        return c
    lax.fori_loop(0, rows, wait, 0)


def _gather_rows(src, idx, rows):
    n, width = src.shape
    p = idx.shape[0]
    return pl.pallas_call(
        _gather_rows_kernel,
        grid=(p // rows,),
        in_specs=[pl.BlockSpec((1, 1, rows), lambda i: (i, 0, 0), memory_space=pltpu.SMEM),
                  pl.BlockSpec(memory_space=pl.ANY)],
        out_specs=pl.BlockSpec((rows, width), lambda i: (i, 0)),
        out_shape=jax.ShapeDtypeStruct((p, width), src.dtype),
        scratch_shapes=[pltpu.SemaphoreType.DMA(())],
        compiler_params=_cparams(1),
        name="gather_rows",
    )(idx.reshape(p // rows, 1, rows), src)


def _moe_ffn_kernel(te_ref, tv_ref, h_ref, wg_ref, wu_ref, wd_ref, o_ref, h_sc):
    i = pl.program_id(0)
    f = pl.program_id(1)

    @pl.when(f == 0)
    def _start():
        o_ref[...] = jnp.zeros(o_ref.shape, F32)
        h_sc[...] = h_ref[...].astype(BF16)

    @pl.when(tv_ref[i] > 0)
    def _compute():
        o_ref[...] += _swiglu_step(h_sc[...], wg_ref, wu_ref, wd_ref)


def _moe_ffn(h_sorted, tile_expert, tile_valid, w_gu, w_down, e_off, tm, tf):
    p, d = h_sorted.shape
    d_ff = w_gu.shape[2] // 2
    nf = d_ff // tf

    def w_map(col0):
        def index(i, f, te, tv):
            return (e_off + te[i], 0, col0 + jnp.where(tv[i] > 0, f, nf - 1))
        return index

    def wd_map(i, f, te, tv):
        return (e_off + te[i], jnp.where(tv[i] > 0, f, nf - 1), 0)

    grid_spec = pltpu.PrefetchScalarGridSpec(
        num_scalar_prefetch=2,
        grid=(p // tm, nf),
        in_specs=[pl.BlockSpec((tm, d), lambda i, f, te, tv: (i, 0)),
                  pl.BlockSpec((1, d, tf), w_map(0)),
                  pl.BlockSpec((1, d, tf), w_map(nf)),
                  pl.BlockSpec((1, tf, d), wd_map)],
        out_specs=pl.BlockSpec((tm, d), lambda i, f, te, tv: (i, 0)),
        scratch_shapes=[pltpu.VMEM((tm, d), BF16)])
    return pl.pallas_call(
        _moe_ffn_kernel,
        grid_spec=grid_spec,
        out_shape=jax.ShapeDtypeStruct((p, d), F32),
        compiler_params=_cparams(2),
        name="moe_ffn",
    )(tile_expert, tile_valid, h_sorted, w_gu, w_gu, w_down)


def _moe_combine_kernel(*refs):
    idx_refs = refs[:TOP_E]
    y_hbm, x_ref, gt_ref, rt_ref, o_ref = refs[TOP_E:TOP_E + 5]
    bufs = refs[TOP_E + 5:2 * TOP_E + 5]
    sem = refs[2 * TOP_E + 5]
    rows = x_ref.shape[1]

    def start(r, c):
        for j in range(TOP_E):
            _row_copy(y_hbm, idx_refs[j][0, 0, r], bufs[j], r, sem.at[j]).start()
        return c
    lax.fori_loop(0, rows, start, 0)

    def wait(r, c):
        for j in range(TOP_E):
            _row_copy(y_hbm, idx_refs[j][0, 0, r], bufs[j], r, sem.at[j]).wait()
        return c
    lax.fori_loop(0, rows, wait, 0)
    route = rt_ref[0]
    y = bufs[0][...] * route[:, TOP_E:TOP_E + 1]
    for j in range(1, TOP_E):
        y = y + bufs[j][...] * route[:, TOP_E + j:TOP_E + j + 1]
    o_ref[0] = x_ref[0] + gt_ref[0] * y


def _moe_combine(y_sorted, dest, route, x, gt, rows, per_row):
    G, T, d = x.shape
    nt = T // rows
    idx = [dest[:, j].reshape(G * nt, 1, rows) for j in range(TOP_E)]
    smem = pl.BlockSpec((1, 1, rows), lambda g, i: (g * nt + i, 0, 0), memory_space=pltpu.SMEM)
    row = pl.BlockSpec((1, rows, d), lambda g, i: (g, i, 0))
    return pl.pallas_call(
        _moe_combine_kernel,
        grid=(G, nt),
        in_specs=[smem] * TOP_E + [pl.BlockSpec(memory_space=pl.ANY), row, _mod_spec(rows, d, per_row),
                                   pl.BlockSpec((1, rows, LANES), lambda g, i: (g, i, 0))],
        out_specs=row,
        out_shape=jax.ShapeDtypeStruct((G, T, d), F32),
        scratch_shapes=[pltpu.VMEM((rows, d), F32) for _ in range(TOP_E)]
                       + [pltpu.SemaphoreType.DMA((TOP_E,))],
        compiler_params=_cparams(2),
        name="moe_combine",
    )(*idx, y_sorted, x, gt, route)


def _final_norm_kernel(x_ref, g_ref, o_ref):
    x = x_ref[0]
    inv = lax.rsqrt(jnp.mean(x * x, axis=-1, keepdims=True) + EPS)
    o_ref[0] = (x * inv) * g_ref[...]


def _final_norm(x, g, tm):
    G, T, d = x.shape
    row = pl.BlockSpec((1, tm, d), lambda g_, i: (g_, i, 0))
    return pl.pallas_call(
        _final_norm_kernel, grid=(G, T // tm),
        in_specs=[row, pl.BlockSpec(g.shape, lambda g_, i: (0, 0))], out_specs=row,
        out_shape=jax.ShapeDtypeStruct((G, T, d), F32),
        compiler_params=_cparams(2), name="final_norm",
    )(x, g)


def _largest_tile(n, cap):
    t = min(n, cap)
    while n % t:
        t //= 2
    return t


def kernel(x_prompt, x_sample, cache_k, cache_v, cache_kidx, state_conv, page_table, c_prompt, c_sample,
           w_ada, b_ada, g_norm_mix, g_norm_ffn, g_norm_final, w_attn_in, w_attn_out,
           w_conv_in, w_conv, w_conv_out, w_ffn_gate_up, w_ffn_down, w_router, b_router,
           w_moe_gate_up, w_moe_down):
    B, T, d = x_prompt.shape
    DB, Tn, _ = x_sample.shape
    assert Tn == 1
    depth = w_ada.shape[0]
    n_attn, n_pool, page, n_heads, head_dim = cache_k.shape
    idx_dim = cache_kidx.shape[-1]
    n_ih = (w_attn_in.shape[-1] - 3 * d - idx_dim) // (idx_dim + 1)
    n_pages = page_table.shape[1]
    past = n_pages * page
    n_experts = w_router.shape[-1]
    conv_w = w_conv.shape[1]
    topk_p = min(TOPK_MAX, T // 4)
    topk_s = min(TOPK_MAX, (past + Tn) // 4)

    tm_p = _largest_tile(T, 512)
    tm_f = _largest_tile(T, 1024)
    tf = _largest_tile(w_ffn_gate_up.shape[-1] // 2, 512)
    pg = _largest_tile(n_pages, 8)
    tk = _largest_tile(T, 512)
    qb = _largest_tile(T, 256)
    rows_g = _largest_tile(T, 256)

    r = B + DB
    r_pad = -(-r // 8) * 8
    c_all = jnp.concatenate([c_prompt, c_sample, jnp.zeros((r_pad - r, d), F32)], axis=0)
    mod = _adaln(c_all, w_ada, b_ada)

    w_moe_gu = w_moe_gate_up.reshape((-1,) + w_moe_gate_up.shape[2:])
    w_moe_dn = w_moe_down.reshape((-1,) + w_moe_down.shape[2:])
    ck_t = jnp.transpose(cache_k, (0, 1, 3, 4, 2))
    cv_t = jnp.transpose(cache_v, (0, 1, 3, 4, 2))
    ckidx_t = jnp.swapaxes(cache_kidx, 2, 3)
    xp = x_prompt
    xs = x_sample.reshape(1, DB, d)

    outs = {n: [] for n in ("kp", "vp", "ip", "cp", "ks", "vs", "is", "cs")}
    for l in range(depth):
        j = l // 2
        mp = [mod[l, :B, i * d:(i + 1) * d].reshape(B, 1, d) for i in range(6)]
        ms = [mod[l, B:r, i * d:(i + 1) * d].reshape(1, DB, d) for i in range(6)]
        g_mix = g_norm_mix[l].reshape(1, d)
        g_ffn = g_norm_ffn[l].reshape(1, d)
        if l % 2 == 0:
            wi = w_attn_in[j].astype(BF16)
            nkw = idx_dim + n_ih
            wkw = jnp.pad(wi[:, 3 * d + n_ih * idx_dim:], ((0, 0), (0, LANES - nkw)))
            ws = (wi[:, :d], wi[:, d:2 * d], wi[:, 2 * d:3 * d],
                  wi[:, 3 * d:3 * d + n_ih * idx_dim], wkw)
            wo = w_attn_out[j].astype(BF16)
            q, kt, vt, kbT, vb, qi, kw, kwt = _attn_in(xp, mp[0], mp[1], g_mix, ws, tm_p, False,
                                                       head_dim ** -0.5 * LOG2_E, idx_dim)
            widx = kw[:, :, idx_dim:nkw]
            kidx_t = kwt[:, :idx_dim, :]
            att = _prompt_attn(q, qi, widx, kidx_t.astype(BF16), kbT, vb, qb=qb, tk=tk, topk=topk_p,
                               n_heads=n_heads, n_ih=n_ih)
            xp = _proj_res(att, wo, xp, mp[2], tm_p, False)
            tokens_first = lambda a: jnp.moveaxis(a.reshape(B, n_heads, head_dim, T), 3, 1)
            outs["kp"].append(tokens_first(kt))
            outs["vp"].append(tokens_first(vt))
            outs["ip"].append(jnp.swapaxes(kidx_t, 1, 2))
            q, kt, vt, _, _, qi, kw, _ = _attn_in(xs, ms[0], ms[1], g_mix, ws, DB, True,
                                                  head_dim ** -0.5, idx_dim)
            k = kt[0].T
            v = vt[0].T
            kidx = kw[0, :, :idx_dim]
            widx = kw[0, :, idx_dim:nkw]
            sc_past = _sample_scores(page_table, qi.reshape(DB, n_ih, idx_dim),
                                     (widx * n_ih ** -0.5).reshape(DB, n_ih, 1), ckidx_t, j, pg)
            bias = _sample_select(sc_past, qi[0], kidx, widx, topk_s, n_ih)
            heads = lambda a: a.astype(F32).reshape(DB, n_heads, head_dim)
            att = _sample_attn(page_table, heads(q), heads(k), heads(v), bias, ck_t, cv_t, j, pg)
            xs = _proj_res(att.astype(BF16).reshape(1, DB, d), wo, xs, ms[2], DB, True)
            outs["ks"].append(k.reshape(DB, Tn, n_heads, head_dim))
            outs["vs"].append(v.reshape(DB, Tn, n_heads, head_dim))
            outs["is"].append(kidx.reshape(DB, Tn, idx_dim))
            hp = _norm(xp, mp[3], mp[4], g_ffn, tm_p, False)
            xp = _ffn(hp, xp, mp[5], w_ffn_gate_up, w_ffn_down, j, tm_f, tf, False)
            hs = _norm(xs, ms[3], ms[4], g_ffn, DB, True)
            xs = _ffn(hs, xs, ms[5], w_ffn_gate_up, w_ffn_down, j, DB, tf, True)
        else:
            wi = w_conv_in[j].astype(BF16)
            wb, wc, wx = wi[:, :d], wi[:, d:2 * d], wi[:, 2 * d:]
            wo = w_conv_out[j].astype(BF16)
            xp, st = _conv_prompt(xp, mp[0], mp[1], mp[2], g_mix, wb, wc, wx, w_conv[j], wo, tm_p)
            outs["cp"].append(st)
            xs2, ns = _conv_sample(xs[0], ms[0][0], ms[1][0], ms[2][0], g_mix,
                                   jnp.swapaxes(state_conv[j], 0, 1), wb, wc, wx, w_conv[j], wo)
            xs = xs2.reshape(1, DB, d)
            outs["cs"].append(jnp.swapaxes(ns, 0, 1))
            wr = jnp.pad(w_router[j], ((0, 0), (0, LANES - n_experts)))
            br = jnp.pad(b_router[j], (0, LANES - n_experts)).reshape(1, LANES)
            router = (wr, br, n_experts)
            hp, rp = _norm(xp, mp[3], mp[4], g_ffn, tm_p, False, router)
            hs, rs = _norm(xs, ms[3], ms[4], g_ffn, DB, True, router)
            h_all = jnp.concatenate([hp.reshape(B * T, d), hs.reshape(DB, d)], axis=0)
            route = jnp.concatenate([rp.reshape(B * T, LANES), rs.reshape(DB, LANES)], axis=0)
            src, dest, t_exp, t_val = _route_plan(route, n_experts, tm_f)
            h_sorted = _gather_rows(h_all, src, rows_g)
            y_sorted = _moe_ffn(h_sorted, t_exp, t_val, w_moe_gu, w_moe_dn, j * n_experts, tm_f, tf)
            xp = _moe_combine(y_sorted, dest[:B * T], rp, xp, mp[5], rows_g, False)
            xs = _moe_combine(y_sorted, dest[B * T:], rs, xs, ms[5], _largest_tile(DB, rows_g), True)

    gf = g_norm_final.reshape(1, d)
    y_prompt = _final_norm(xp, gf, tm_p)
    y_sample = _final_norm(xs, gf, DB).reshape(DB, Tn, d)
    st = lambda n: jnp.stack(outs[n])
    return (y_prompt, y_sample, st("kp"), st("vp"), st("ip"), st("cp"),
            st("ks"), st("vs"), st("is"), st("cs"))
```

```python
import functools

import jax
import jax.numpy as jnp
from jax import lax
from jax.experimental import pallas as pl
from jax.experimental.pallas import tpu as pltpu

F32 = jnp.float32
BF16 = jnp.bfloat16
I32 = jnp.int32

EPS = 1e-6
LOG2_E = 1.4426950408889634
TOPK_MAX = 256
TOP_E = 2
LANES = 128
NEG_BIAS = -1e30
INT_MIN = -(2 ** 31)
KEY_NEG_INF = -2139095041
KEY_POS_INF = 2139095040
VMEM_LIMIT = 56 * 1024 * 1024
SLAB_UNROLL = 4
DMA_QUEUES = 2
HEAD_GROUP = 4

def _cparams(n_axes):
    return pltpu.CompilerParams(dimension_semantics=("arbitrary",) * n_axes,
                                vmem_limit_bytes=VMEM_LIMIT)


def _norm_mod(x, g, sh, sc):
    inv = lax.rsqrt(jnp.mean(x * x, axis=-1, keepdims=True) + EPS)
    return ((x * inv) * g) * (1.0 + sc) + sh


def _silu(x):
    return x * (1.0 / (1.0 + jnp.exp(-x)))


def _float_key(s):
    bits = lax.bitcast_convert_type(s, I32)
    return jnp.where(bits < 0, bits ^ jnp.int32(0x7FFFFFFF), bits)


def _adaln_kernel(c_ref, w_ref, b_ref, o_ref):
    a = _silu(c_ref[...]).astype(BF16)
    o_ref[0] = jnp.dot(a, w_ref[0].astype(BF16), preferred_element_type=F32) + b_ref[0]


def _adaln(c_all, w_ada, b_ada):
    depth, d, n = w_ada.shape
    r = c_all.shape[0]
    tn = n // 4
    return pl.pallas_call(
        _adaln_kernel,
        grid=(depth, n // tn),
        in_specs=[pl.BlockSpec((r, d), lambda l, j: (0, 0)),
                  pl.BlockSpec((1, d, tn), lambda l, j: (l, 0, j)),
                  pl.BlockSpec((1, 1, tn), lambda l, j: (l, 0, j))],
        out_specs=pl.BlockSpec((1, r, tn), lambda l, j: (l, 0, j)),
        out_shape=jax.ShapeDtypeStruct((depth, r, n), F32),
        compiler_params=_cparams(2),
        name="adaln",
    )(c_all, w_ada, b_ada.reshape(depth, 1, n))


def _mod_spec(tm, d, per_row):
    if per_row:
        return pl.BlockSpec((1, tm, d), lambda g, i, *_: (g, i, 0))
    return pl.BlockSpec((1, 1, d), lambda g, i, *_: (g, 0, 0))


def _dot_nt(a, b):
    return lax.dot_general(a, b, (((1,), (1,)), ((), ())), preferred_element_type=F32)


def _attn_in_kernel(x_ref, sh_ref, sc_ref, g_ref, wq_ref, wqt_ref, wk_ref, wkt_ref, wvt_ref, wqi_ref,
                    wkw_ref, wkwt_ref, q_ref, qt_ref, kt_ref, vt_ref, kb_ref, vb_ref, qi_ref, kw_ref,
                    kwt_ref, *, q_scale, qi_scale):
    h = _norm_mod(x_ref[0], g_ref[...], sh_ref[0], sc_ref[0]).astype(BF16)
    q_ref[0] = (jnp.dot(h, wq_ref[...], preferred_element_type=F32) * q_scale).astype(BF16)
    qt_ref[0] = (_dot_nt(wqt_ref[...], h) * q_scale).astype(BF16)
    kt_ref[0] = _dot_nt(wkt_ref[...], h)
    kb_ref[0] = jnp.dot(h, wk_ref[...], preferred_element_type=F32).astype(BF16)
    vt = _dot_nt(wvt_ref[...], h)
    vt_ref[0] = vt
    vb_ref[0] = vt.astype(BF16)
    qi_ref[0] = (jnp.dot(h, wqi_ref[...], preferred_element_type=F32) * qi_scale).astype(BF16)
    kw_ref[0] = jnp.dot(h, wkw_ref[...], preferred_element_type=F32)
    kwt_ref[0] = _dot_nt(wkwt_ref[...], h)


def _attn_in(x, sh, sc, g, ws, tm, per_row, q_scale, idx_dim):
    G, T, d = x.shape
    wq, wk, wv, wqi, wkw = ws
    ws = (wq, wq.T, wk, wk.T, wv.T, wqi, wkw, wkw.T)
    nqi = wqi.shape[1]
    nkw = wkw.shape[1]
    full = lambda a: pl.BlockSpec(a.shape, lambda g_, i: (0,) * a.ndim)
    row = lambda n: pl.BlockSpec((1, tm, n), lambda g_, i: (g_, i, 0))
    col = lambda n: pl.BlockSpec((1, n, tm), lambda g_, i: (g_, 0, i))
    kern = functools.partial(_attn_in_kernel, q_scale=q_scale, qi_scale=idx_dim ** -0.5)
    return pl.pallas_call(
        kern,
        grid=(G, T // tm),
        in_specs=[row(d), _mod_spec(tm, d, per_row), _mod_spec(tm, d, per_row), full(g)]
                 + [full(w) for w in ws],
        out_specs=[row(d), col(d), col(d), col(d), row(d), col(d), row(nqi), row(nkw), col(nkw)],
        out_shape=[jax.ShapeDtypeStruct((G, T, d), BF16),
                   jax.ShapeDtypeStruct((G, d, T), BF16),
                   jax.ShapeDtypeStruct((G, d, T), F32),
                   jax.ShapeDtypeStruct((G, d, T), F32),
                   jax.ShapeDtypeStruct((G, T, d), BF16),
                   jax.ShapeDtypeStruct((G, d, T), BF16),
                   jax.ShapeDtypeStruct((G, T, nqi), BF16),
                   jax.ShapeDtypeStruct((G, T, nkw), F32),
                   jax.ShapeDtypeStruct((G, nkw, T), F32)],
        compiler_params=_cparams(2),
        name="attn_in",
    )(x, sh, sc, g, *ws)


def _count_groups(key_sc, n_slabs, groups, row_group, preds):
    def add(cnt, c, r0, pred):
        kk = key_sc[c, r0:r0 + row_group, :]
        return cnt + jnp.where(pred(kk, c), 1, 0).astype(I32)

    parts = []
    for r0, pred in zip(groups, preds):
        cnt = jnp.zeros((row_group, LANES), I32)
        if isinstance(n_slabs, int):
            for c in range(n_slabs):
                cnt = add(cnt, c, r0, pred)
        else:
            def body(cc, cnt, r0=r0, pred=pred):
                for u in range(SLAB_UNROLL):
                    cnt = add(cnt, cc * SLAB_UNROLL + u, r0, pred)
                return cnt
            cnt = lax.fori_loop(0, n_slabs // SLAB_UNROLL, body, cnt)
        parts.append(cnt)
    return [jnp.sum(p, axis=1, keepdims=True) for p in parts]


def _select_bias(key_sc, bias_sc, n_slabs, rows, row_group, topk):
    groups = list(range(0, rows, row_group))
    ng = len(groups)
    lane = lax.broadcasted_iota(I32, (row_group, LANES), 1)
    zeros = tuple(jnp.zeros((row_group, LANES), I32) for _ in groups)

    def bit_body(i, prefixes):
        bit = lax.shift_left(jnp.int32(1), 31 - i)
        trials = [p | bit for p in prefixes]
        preds = [lambda kk, c, ts=t ^ jnp.int32(INT_MIN): kk >= ts for t in trials]
        cnts = _count_groups(key_sc, n_slabs, groups, row_group, preds)
        return tuple(jnp.where(cnts[g] >= topk, trials[g], prefixes[g]) for g in range(ng))

    prefixes = lax.fori_loop(0, 32, bit_body, zeros)
    thrs = [p ^ jnp.int32(INT_MIN) for p in prefixes]
    n_gt = _count_groups(key_sc, n_slabs, groups, row_group,
                         [lambda kk, c, t=t: kk > t for t in thrs])
    n_eq = _count_groups(key_sc, n_slabs, groups, row_group,
                         [lambda kk, c, t=t: kk == t for t in thrs])
    needs = [topk - n for n in n_gt]
    tie = jnp.int32(0)
    for g in range(ng):
        tie = jnp.maximum(tie, jnp.max(jnp.where(n_eq[g] > needs[g], 1, 0)))

    def tie_cut():
        def cut_body(i, cuts):
            bit = lax.shift_left(jnp.int32(1), 30 - i)
            trials = [p | bit for p in cuts]
            preds = [lambda kk, c, t=thrs[g], tr=trials[g]: (kk == t) & (c * LANES + lane < tr)
                     for g in range(ng)]
            cnts = _count_groups(key_sc, n_slabs, groups, row_group, preds)
            return tuple(jnp.where(cnts[g] < needs[g], trials[g], cuts[g]) for g in range(ng))
        return lax.fori_loop(0, 31, cut_body, zeros)

    cuts = lax.cond(tie > 0, tie_cut,
                    lambda: tuple(jnp.full((row_group, LANES), 2 ** 31 - 1, I32) for _ in groups))

    def bias_body(c, carry):
        col = c * LANES + lane
        for g, r0 in enumerate(groups):
            kk = key_sc[c, r0:r0 + row_group, :]
            sel = (kk > thrs[g]) | ((kk == thrs[g]) & (col <= cuts[g]))
            fin = (kk > KEY_NEG_INF) & (kk < KEY_POS_INF)
            bias_sc[c, r0:r0 + row_group, :] = jnp.where(sel & fin, 0.0, NEG_BIAS).astype(F32)
        return carry
    lax.fori_loop(0, n_slabs, bias_body, 0)


def _prompt_attn_kernel(q_ref, qi_ref, w_ref, kT_ref, k_ref, v_ref, o_ref,
                        key_sc, bias_sc, m_sc, l_sc, acc_sc,
                        *, qb, tk, topk, n_heads, head_dim, n_ih, idx_dim, w_scale, row_group):
    qi_blk = pl.program_id(1)
    kj = pl.program_id(2)
    nk = pl.num_programs(2)
    n_valid = (qi_blk + 1) * qb
    n_chunks = (n_valid + tk - 1) // tk
    spc = tk // LANES

    @pl.when(kj == 0)
    def _indexer():
        qi = qi_ref[0]
        w = w_ref[0] * w_scale
        wcols = [w[:, h:h + 1] for h in range(n_ih)]
        row = qi_blk * qb + lax.broadcasted_iota(I32, (qb, tk), 0)
        lane = lax.broadcasted_iota(I32, (qb, tk), 1)

        def chunk_body(c, carry):
            kT = kT_ref[0, c]
            s = None
            for h in range(n_ih):
                d = jnp.dot(qi[:, h * idx_dim:(h + 1) * idx_dim], kT, preferred_element_type=F32)
                t = jnp.maximum(d, 0.0) * wcols[h]
                s = t if s is None else s + t
            key = jnp.where(c * tk + lane <= row, _float_key(s), jnp.int32(INT_MIN))
            for j in range(spc):
                key_sc[c * spc + j] = key[:, j * LANES:(j + 1) * LANES]
            return carry
        lax.fori_loop(0, n_chunks, chunk_body, 0)
        _select_bias(key_sc, bias_sc, n_chunks * spc, qb, row_group, topk)
        m_sc[...] = jnp.full(m_sc.shape, NEG_BIAS, F32)
        l_sc[...] = jnp.zeros(l_sc.shape, F32)
        acc_sc[...] = jnp.zeros(acc_sc.shape, F32)

    @pl.when(kj * tk < n_valid)
    def _attend():
        bias_t = jnp.concatenate([bias_sc[kj * spc + j] for j in range(spc)], axis=1).T
        hsl = lambda h: slice(h * head_dim, (h + 1) * head_dim)
        for h0 in range(0, n_heads, HEAD_GROUP):
            hh = range(h0, h0 + HEAD_GROUP)
            ss = [jnp.dot(k_ref[0, h], q_ref[0, hsl(h), :], preferred_element_type=F32) + bias_t
                  for h in hh]
            m_new = [jnp.maximum(m_sc[h], jnp.max(s, axis=0, keepdims=True)) for h, s in zip(hh, ss)]
            alpha = [jnp.exp2(m_sc[h] - m) for h, m in zip(hh, m_new)]
            ps = [jnp.exp2(s - m) for s, m in zip(ss, m_new)]
            for i, h in enumerate(hh):
                l_sc[h] = alpha[i] * l_sc[h] + jnp.sum(ps[i], axis=0, keepdims=True)
                pv = jnp.dot(v_ref[0, hsl(h), :], ps[i].astype(BF16), preferred_element_type=F32)
                acc_sc[h] = acc_sc[h] * alpha[i] + pv
                m_sc[h] = m_new[i]

    @pl.when(kj == nk - 1)
    def _finish():
        for h in range(n_heads):
            o_ref[0, h * head_dim:(h + 1) * head_dim, :] = (
                acc_sc[h] * (1.0 / l_sc[h])).astype(o_ref.dtype)


def _prompt_attn(q_t, qi, w, kT, kb, vb_t, *, qb, tk, topk, n_heads, n_ih):
    B, d, T = q_t.shape
    head_dim = d // n_heads
    idx_dim = qi.shape[2] // n_ih
    nq, nk = T // qb, T // tk
    assert qb >= topk and tk % (LANES * SLAB_UNROLL) == 0 and qb <= tk
    kT = kT.reshape(B, idx_dim, nk, tk).transpose(0, 2, 1, 3)

    def last_tile(i, j):
        return jnp.minimum(j, ((i + 1) * qb - 1) // tk)

    kern = functools.partial(_prompt_attn_kernel, qb=qb, tk=tk, topk=topk, n_heads=n_heads,
                             head_dim=head_dim, n_ih=n_ih, idx_dim=idx_dim,
                             w_scale=n_ih ** -0.5, row_group=64)
    return pl.pallas_call(
        kern,
        grid=(B, nq, nk),
        in_specs=[pl.BlockSpec((1, d, qb), lambda b, i, j: (b, 0, i)),
                  pl.BlockSpec((1, qb, qi.shape[2]), lambda b, i, j: (b, i, 0)),
                  pl.BlockSpec((1, qb, w.shape[2]), lambda b, i, j: (b, i, 0)),
                  pl.BlockSpec((1, nk, idx_dim, tk), lambda b, i, j: (b, 0, 0, 0)),
                  pl.BlockSpec((1, n_heads, tk, head_dim), lambda b, i, j: (b, 0, last_tile(i, j), 0)),
                  pl.BlockSpec((1, d, tk), lambda b, i, j: (b, 0, last_tile(i, j)))],
        out_specs=pl.BlockSpec((1, d, qb), lambda b, i, j: (b, 0, i)),
        out_shape=jax.ShapeDtypeStruct((B, d, T), BF16),
        scratch_shapes=[pltpu.VMEM((T // LANES, qb, LANES), I32),
                        pltpu.VMEM((T // LANES, qb, LANES), F32),
                        pltpu.VMEM((n_heads, 1, qb), F32),
                        pltpu.VMEM((n_heads, 1, qb), F32),
                        pltpu.VMEM((n_heads, head_dim, qb), F32)],
        compiler_params=_cparams(3),
        name="prompt_attn",
    )(q_t, qi, w, kT, kb.reshape(B, T, n_heads, head_dim).transpose(0, 2, 1, 3), vb_t)


def _sample_scores_kernel(pt_ref, qi_ref, w_ref, *rest, pg):
    o_ref = rest[pg]
    qi = qi_ref[0]
    w = w_ref[0]
    for p in range(pg):
        kp = rest[p][0, 0].astype(BF16).astype(F32)
        d = jnp.dot(qi.astype(F32), kp, preferred_element_type=F32)
        o_ref[0, 0, p:p + 1, :] = jnp.sum(jnp.maximum(d, 0.0) * w, axis=0, keepdims=True)


def _sample_scores(page_table, qi3, w3, cache_kidx_t, layer, pg):
    DB, n_pages = page_table.shape
    _, _, idx_dim, page = cache_kidx_t.shape
    n_ih = qi3.shape[1]
    page_specs = [
        pl.BlockSpec((1, 1, idx_dim, page),
                     lambda b, j, pt, p=p: (layer, pt[b, j * pg + p], 0, 0))
        for p in range(pg)]
    grid_spec = pltpu.PrefetchScalarGridSpec(
        num_scalar_prefetch=1,
        grid=(DB, n_pages // pg),
        in_specs=[pl.BlockSpec((1, n_ih, idx_dim), lambda b, j, pt: (b, 0, 0)),
                  pl.BlockSpec((1, n_ih, 1), lambda b, j, pt: (b, 0, 0))] + page_specs,
        out_specs=pl.BlockSpec((1, 1, pg, page), lambda b, j, pt: (b, j, 0, 0)))
    out = pl.pallas_call(
        functools.partial(_sample_scores_kernel, pg=pg),
        grid_spec=grid_spec,
        out_shape=jax.ShapeDtypeStruct((DB, n_pages // pg, pg, page), F32),
        compiler_params=_cparams(2),
        name="sample_scores",
    )(page_table, qi3, w3, *([cache_kidx_t] * pg))
    return out.reshape(DB, n_pages * page)


def _sample_select_kernel(sp_ref, qi_ref, kn_ref, w_ref, bias_ref, key_sc, *, topk, n_ih, idx_dim,
                          w_scale, row_group):
    DB, past = sp_ref.shape
    n_past = past // LANES
    qi = qi_ref[...].astype(F32)
    kn = kn_ref[...]
    w = w_ref[...] * w_scale
    s_new = None
    for h in range(n_ih):
        d = jnp.sum(qi[:, h * idx_dim:(h + 1) * idx_dim] * kn, axis=1, keepdims=True)
        t = jnp.maximum(d, 0.0) * w[:, h:h + 1]
        s_new = t if s_new is None else s_new + t
    for c in range(n_past):
        key_sc[c] = _float_key(sp_ref[:, c * LANES:(c + 1) * LANES])
    lane = lax.broadcasted_iota(I32, (DB, LANES), 1)
    key_sc[n_past] = jnp.where(lane == 0, _float_key(jnp.broadcast_to(s_new, (DB, LANES))),
                               jnp.int32(INT_MIN))
    _select_bias(key_sc, bias_ref, n_past + 1, DB, row_group, topk)


def _sample_select(scores_past, qi, kn, w, topk, n_ih):
    DB, past = scores_past.shape
    idx_dim = kn.shape[1]
    n_slabs = past // LANES + 1
    kern = functools.partial(_sample_select_kernel, topk=topk, n_ih=n_ih, idx_dim=idx_dim,
                             w_scale=n_ih ** -0.5, row_group=min(64, DB))
    return pl.pallas_call(
        kern,
        out_shape=jax.ShapeDtypeStruct((n_slabs, DB, LANES), F32),
        scratch_shapes=[pltpu.VMEM((n_slabs, DB, LANES), I32)],
        compiler_params=pltpu.CompilerParams(vmem_limit_bytes=VMEM_LIMIT),
        name="sample_select",
    )(scores_past, qi, kn, w)


def _sample_attn_kernel(pt_ref, q_ref, kn_ref, vn_ref, bn_ref, bias_ref, *rest, pg, n_heads, head_dim):
    k_refs = rest[:pg]
    v_refs = rest[pg:2 * pg]
    o_ref = rest[2 * pg]
    m_sc, l_sc, acc_sc = rest[2 * pg + 1:]
    jg = pl.program_id(1)
    page = k_refs[0].shape[4]
    q = q_ref[0]

    def update(k_tiles, v_tiles, bias):
        s = jnp.concatenate([jnp.sum(kt * q, axis=1) for kt in k_tiles], axis=1) + bias
        m_prev = m_sc[...]
        m_new = jnp.maximum(m_prev, jnp.max(s, axis=1, keepdims=True))
        alpha = jnp.exp(m_prev - m_new)
        pr = jnp.exp(s - m_new)
        l_sc[...] = alpha * l_sc[...] + jnp.sum(pr, axis=1, keepdims=True)
        m_sc[...] = m_new
        for h in range(n_heads):
            acc = acc_sc[h] * alpha[h:h + 1, :]
            for i, vt in enumerate(v_tiles):
                acc = acc + vt[h] * pr[h:h + 1, i * LANES:(i + 1) * LANES]
            acc_sc[h] = acc

    @pl.when(jg == 0)
    def _init():
        m_sc[...] = jnp.full(m_sc.shape, NEG_BIAS, F32)
        l_sc[...] = jnp.zeros(l_sc.shape, F32)
        acc_sc[...] = jnp.zeros(acc_sc.shape, F32)
        update([kn_ref[0]], [vn_ref[0]], bn_ref[0])

    for p0 in range(0, pg, 2):
        tiles = range(p0, min(p0 + 2, pg))
        update([k_refs[p][0, 0] for p in tiles], [v_refs[p][0, 0] for p in tiles],
               bias_ref[0][:, p0 * page:(p0 + len(tiles)) * page])

    @pl.when(jg == pl.num_programs(1) - 1)
    def _finish():
        inv = 1.0 / l_sc[...]
        for h in range(n_heads):
            o_ref[0, h] = jnp.sum(acc_sc[h], axis=1, keepdims=True) * inv[h:h + 1, :]


def _sample_attn(page_table, q, kn, vn, bias, cache_k_t, cache_v_t, layer, pg):
    DB, n_pages = page_table.shape
    _, _, n_heads, head_dim, page = cache_k_t.shape
    n_slabs = bias.shape[0]
    past = n_pages * page
    assert page == LANES and n_slabs * LANES == past + LANES
    bias_rows = bias.transpose(1, 0, 2).reshape(DB, 1, n_slabs * LANES)
    lanes = lambda a: jnp.broadcast_to(a[..., None], a.shape + (LANES,))
    blk = pg * page
    page_spec = lambda p: pl.BlockSpec(
        (1, 1, n_heads, head_dim, page), lambda b, j, pt, p=p: (layer, pt[b, j * pg + p], 0, 0, 0))
    head = pl.BlockSpec((1, n_heads, head_dim, LANES), lambda b, j, pt: (b, 0, 0, 0))
    grid_spec = pltpu.PrefetchScalarGridSpec(
        num_scalar_prefetch=1,
        grid=(DB, n_pages // pg),
        in_specs=[head, head, head,
                  pl.BlockSpec((1, 1, LANES), lambda b, j, pt: (b, 0, past // LANES)),
                  pl.BlockSpec((1, 1, blk), lambda b, j, pt: (b, 0, j))]
                 + [page_spec(p) for p in range(pg)] * 2,
        out_specs=pl.BlockSpec((1, n_heads, head_dim, 1), lambda b, j, pt: (b, 0, 0, 0)),
        scratch_shapes=[pltpu.VMEM((n_heads, 1), F32),
                        pltpu.VMEM((n_heads, 1), F32),
                        pltpu.VMEM((n_heads, head_dim, LANES), F32)])
    out = pl.pallas_call(
        functools.partial(_sample_attn_kernel, pg=pg, n_heads=n_heads, head_dim=head_dim),
        grid_spec=grid_spec,
        out_shape=jax.ShapeDtypeStruct((DB, n_heads, head_dim, 1), F32),
        compiler_params=_cparams(2),
        name="sample_attn",
    )(page_table, lanes(q), lanes(kn), lanes(vn), bias_rows, bias_rows,
      *([cache_k_t] * pg), *([cache_v_t] * pg))
    return out.reshape(DB, n_heads * head_dim)


def _proj_res_kernel(z_ref, w_ref, x_ref, gt_ref, o_ref, *, z_transposed):
    z = z_ref[0]
    if z_transposed:
        z = z.astype(F32).T.astype(BF16)
    y = jnp.dot(z, w_ref[...], preferred_element_type=F32)
    o_ref[0] = x_ref[0] + gt_ref[0] * y


def _proj_res(z, w, x, gt, tm, per_row, z_transposed=False):
    G, T, d = x.shape
    row = pl.BlockSpec((1, tm, d), lambda g, i: (g, i, 0))
    z_spec = pl.BlockSpec((1, d, tm), lambda g, i: (g, 0, i)) if z_transposed else row
    return pl.pallas_call(
        functools.partial(_proj_res_kernel, z_transposed=z_transposed),
        grid=(G, T // tm),
        in_specs=[z_spec, pl.BlockSpec(w.shape, lambda g, i: (0, 0)), row, _mod_spec(tm, d, per_row)],
        out_specs=row,
        out_shape=jax.ShapeDtypeStruct((G, T, d), F32),
        compiler_params=_cparams(2),
        name="proj_res",
    )(z, w, x, gt)


def _conv_prompt_kernel(x_ref, sh_ref, sc_ref, gt_ref, g_ref, wb_ref, wc_ref, wx_ref, wcv_ref, wo_ref,
                        o_ref, st_ref, carry_sc, *, conv_w):
    i = pl.program_id(1)
    tm = x_ref.shape[1]
    x = x_ref[0]
    h = _norm_mod(x, g_ref[...], sh_ref[0], sc_ref[0]).astype(BF16)
    gb = jnp.dot(h, wb_ref[...], preferred_element_type=F32)
    u = (jnp.dot(h, wc_ref[...], preferred_element_type=F32)
         * jnp.dot(h, wx_ref[...], preferred_element_type=F32))

    @pl.when(i == 0)
    def _zero_state():
        carry_sc[...] = jnp.zeros(carry_sc.shape, F32)

    rows = lax.broadcasted_iota(I32, u.shape, 0)
    conv = u * wcv_ref[conv_w - 1:conv_w, :]
    for back in range(1, conv_w):
        shifted = pltpu.roll(u, back, axis=0)
        for r in range(back):
            prev = carry_sc[conv_w - 1 - back + r:conv_w - back + r, :]
            shifted = jnp.where(rows == r, prev, shifted)
        conv = conv + shifted * wcv_ref[conv_w - 1 - back:conv_w - back, :]
    carry_sc[...] = u[tm - (conv_w - 1):, :]
    st_ref[0] = u[tm - (conv_w - 1):, :]
    y = jnp.dot((gb * conv).astype(BF16), wo_ref[...], preferred_element_type=F32)
    o_ref[0] = x + gt_ref[0] * y


def _conv_prompt(x, sh, sc, gt, g, wb, wc, wx, wcv, wo, tm):
    B, T, d = x.shape
    conv_w = wcv.shape[0]
    row = pl.BlockSpec((1, tm, d), lambda b, i: (b, i, 0))
    full = lambda a: pl.BlockSpec(a.shape, lambda b, i: (0,) * a.ndim)
    mod = _mod_spec(tm, d, False)
    return pl.pallas_call(
        functools.partial(_conv_prompt_kernel, conv_w=conv_w),
        grid=(B, T // tm),
        in_specs=[row, mod, mod, mod, full(g), full(wb), full(wc), full(wx), full(wcv), full(wo)],
        out_specs=[row, pl.BlockSpec((1, conv_w - 1, d), lambda b, i: (b, 0, 0))],
        out_shape=[jax.ShapeDtypeStruct((B, T, d), F32),
                   jax.ShapeDtypeStruct((B, conv_w - 1, d), F32)],
        scratch_shapes=[pltpu.VMEM((conv_w - 1, d), F32)],
        compiler_params=_cparams(2),
        name="conv_prompt",
    )(x, sh, sc, gt, g, wb, wc, wx, wcv, wo)


def _conv_sample_kernel(x_ref, sh_ref, sc_ref, gt_ref, g_ref, st_ref, wb_ref, wc_ref, wx_ref, wcv_ref,
                        wo_ref, o_ref, ns_ref, *, conv_w):
    x = x_ref[...]
    h = _norm_mod(x, g_ref[...], sh_ref[...], sc_ref[...]).astype(BF16)
    gb = jnp.dot(h, wb_ref[...], preferred_element_type=F32)
    u = (jnp.dot(h, wc_ref[...], preferred_element_type=F32)
         * jnp.dot(h, wx_ref[...], preferred_element_type=F32))
    conv = u * wcv_ref[conv_w - 1:conv_w, :]
    for j in range(conv_w - 1):
        conv = conv + st_ref[j] * wcv_ref[j:j + 1, :]
    for j in range(conv_w - 2):
        ns_ref[j] = st_ref[j + 1]
    ns_ref[conv_w - 2] = u
    y = jnp.dot((gb * conv).astype(BF16), wo_ref[...], preferred_element_type=F32)
    o_ref[...] = x + gt_ref[...] * y


def _conv_sample(x, sh, sc, gt, g, state, wb, wc, wx, wcv, wo):
    DB, d = x.shape
    conv_w = wcv.shape[0]
    return pl.pallas_call(
        functools.partial(_conv_sample_kernel, conv_w=conv_w),
        out_shape=[jax.ShapeDtypeStruct((DB, d), F32),
                   jax.ShapeDtypeStruct((conv_w - 1, DB, d), F32)],
        compiler_params=pltpu.CompilerParams(vmem_limit_bytes=VMEM_LIMIT),
        name="conv_sample",
    )(x, sh, sc, gt, g, state, wb, wc, wx, wcv, wo)


def _norm_kernel(x_ref, sh_ref, sc_ref, g_ref, h_ref):
    h_ref[0] = _norm_mod(x_ref[0], g_ref[...], sh_ref[0], sc_ref[0]).astype(BF16)


def _norm_router_kernel(x_ref, sh_ref, sc_ref, g_ref, wr_ref, br_ref, h_ref, cb_ref, *, n_experts):
    h = _norm_mod(x_ref[0], g_ref[...], sh_ref[0], sc_ref[0])
    h_ref[0] = h
    logits = jnp.dot(h, wr_ref[...], preferred_element_type=F32,
                     precision=lax.Precision.HIGHEST) + br_ref[...]
    lane = lax.broadcasted_iota(I32, logits.shape, 1)
    ninf = jnp.float32(-jnp.inf)
    logits = jnp.where(lane < n_experts, logits, ninf)
    m1 = jnp.max(logits, axis=1, keepdims=True)
    i1 = jnp.min(jnp.where(logits == m1, lane, LANES), axis=1, keepdims=True)
    rest = jnp.where(lane == i1, ninf, logits)
    m2 = jnp.max(rest, axis=1, keepdims=True)
    i2 = jnp.min(jnp.where(rest == m2, lane, LANES), axis=1, keepdims=True)
    e2 = jnp.exp(m2 - m1)
    inv = 1.0 / (1.0 + e2)
    cb_ref[0] = jnp.where(lane == 0, i1.astype(F32), jnp.where(
        lane == 1, i2.astype(F32), jnp.where(lane == 2, inv, jnp.where(lane == 3, e2 * inv, 0.0))))


def _norm(x, sh, sc, g, tm, per_row, router=None):
    G, T, d = x.shape
    row = pl.BlockSpec((1, tm, d), lambda g_, i: (g_, i, 0))
    full = lambda a: pl.BlockSpec(a.shape, lambda g_, i: (0,) * a.ndim)
    mod = _mod_spec(tm, d, per_row)
    if router is None:
        return pl.pallas_call(
            _norm_kernel, grid=(G, T // tm),
            in_specs=[row, mod, mod, full(g)], out_specs=row,
            out_shape=jax.ShapeDtypeStruct((G, T, d), BF16),
            compiler_params=_cparams(2), name="norm",
        )(x, sh, sc, g)
    wr, br, n_experts = router
    return pl.pallas_call(
        functools.partial(_norm_router_kernel, n_experts=n_experts), grid=(G, T // tm),
        in_specs=[row, mod, mod, full(g), full(wr), full(br)],
        out_specs=[row, pl.BlockSpec((1, tm, LANES), lambda g_, i: (g_, i, 0))],
        out_shape=[jax.ShapeDtypeStruct((G, T, d), F32),
                   jax.ShapeDtypeStruct((G, T, LANES), F32)],
        compiler_params=_cparams(2), name="norm_router",
    )(x, sh, sc, g, wr, br)


def _swiglu_step(h, wg_ref, wu_ref, wd_ref):
    gate = jnp.dot(h, wg_ref[0].astype(BF16), preferred_element_type=F32)
    up = jnp.dot(h, wu_ref[0].astype(BF16), preferred_element_type=F32)
    a = _silu(gate) * up
    return jnp.dot(a.astype(BF16), wd_ref[0].astype(BF16), preferred_element_type=F32)


def _ffn_kernel(h_ref, x_ref, gt_ref, wg_ref, wu_ref, wd_ref, o_ref, acc_sc):
    f = pl.program_id(2)

    @pl.when(f == 0)
    def _zero():
        acc_sc[...] = jnp.zeros(acc_sc.shape, F32)

    acc_sc[...] += _swiglu_step(h_ref[0], wg_ref, wu_ref, wd_ref)

    @pl.when(f == pl.num_programs(2) - 1)
    def _finish():
        o_ref[0] = x_ref[0] + gt_ref[0] * acc_sc[...]


def _ffn(h, x, gt, w_gu, w_down, e, tm, tf, per_row):
    G, T, d = x.shape
    d_ff = w_gu.shape[2] // 2
    nf = d_ff // tf
    row = pl.BlockSpec((1, tm, d), lambda g, i, f: (g, i, 0))
    return pl.pallas_call(
        _ffn_kernel,
        grid=(G, T // tm, nf),
        in_specs=[row, row, _mod_spec(tm, d, per_row),
                  pl.BlockSpec((1, d, tf), lambda g, i, f: (e, 0, f)),
                  pl.BlockSpec((1, d, tf), lambda g, i, f: (e, 0, nf + f)),
                  pl.BlockSpec((1, tf, d), lambda g, i, f: (e, f, 0))],
        out_specs=row,
        out_shape=jax.ShapeDtypeStruct((G, T, d), F32),
        scratch_shapes=[pltpu.VMEM((tm, d), F32)],
        compiler_params=_cparams(3),
        name="ffn",
    )(h, x, gt, w_gu, w_gu, w_down)


def _route_plan(route, n_experts, tm):
    n = route.shape[0]
    flat_e = route[:, :TOP_E].astype(I32).reshape(-1)
    onehot = (flat_e[:, None] == jnp.arange(n_experts, dtype=I32)[None, :]).astype(I32)
    rank = jnp.sum((jnp.cumsum(onehot, axis=0) - onehot) * onehot, axis=1)
    counts = jnp.sum(onehot, axis=0)
    padded = (counts + tm - 1) // tm * tm
    ends = jnp.cumsum(padded)
    dest = (ends - padded)[flat_e] + rank
    n_tiles = (TOP_E * n + n_experts * (tm - 1) + tm - 1) // tm
    p_rows = n_tiles * tm
    src = jnp.zeros((p_rows,), I32).at[dest].set(jnp.arange(TOP_E * n, dtype=I32) // TOP_E)
    tile_start = jnp.arange(n_tiles, dtype=I32) * tm
    tile_expert = jnp.minimum(jnp.sum((ends[None, :] <= tile_start[:, None]).astype(I32), axis=1),
                              n_experts - 1)
    tile_valid = (tile_start < ends[-1]).astype(I32)
    return src, dest.reshape(n, TOP_E), tile_expert, tile_valid


def _row_copy(src_hbm, idx, dst, r, sem):
    return pltpu.make_async_copy(src_hbm.at[pl.ds(idx, 1)], dst.at[pl.ds(r, 1)], sem)


def _gather_rows_kernel(idx_ref, src_hbm, o_ref, sem):
    rows = o_ref.shape[0]

    def start(rr, c):
        for u in range(DMA_QUEUES):
            r = rr * DMA_QUEUES + u
            _row_copy(src_hbm, idx_ref[0, 0, r], o_ref, r, sem).start(priority=u)
        return c
    lax.fori_loop(0, rows // DMA_QUEUES, start, 0)

    def wait(r, c):
        _row_copy(src_hbm, idx_ref[0, 0, r], o_ref, r, sem).wait()
        return c
    lax.fori_loop(0, rows, wait, 0)


def _gather_rows(src, idx, rows):
    n, width = src.shape
    p = idx.shape[0]
    return pl.pallas_call(
        _gather_rows_kernel,
        grid=(p // rows,),
        in_specs=[pl.BlockSpec((1, 1, rows), lambda i: (i, 0, 0), memory_space=pltpu.SMEM),
                  pl.BlockSpec(memory_space=pl.ANY)],
        out_specs=pl.BlockSpec((rows, width), lambda i: (i, 0)),
        out_shape=jax.ShapeDtypeStruct((p, width), src.dtype),
        scratch_shapes=[pltpu.SemaphoreType.DMA(())],
        compiler_params=_cparams(1),
        name="gather_rows",
    )(idx.reshape(p // rows, 1, rows), src)


def _moe_ffn_kernel(te_ref, tv_ref, h_ref, wg_ref, wu_ref, wd_ref, o_ref, h_sc):
    i = pl.program_id(0)
    f = pl.program_id(1)

    @pl.when(f == 0)
    def _start():
        o_ref[...] = jnp.zeros(o_ref.shape, F32)
        h_sc[...] = h_ref[...].astype(BF16)

    @pl.when(tv_ref[i] > 0)
    def _compute():
        o_ref[...] += _swiglu_step(h_sc[...], wg_ref, wu_ref, wd_ref)


def _moe_ffn(h_sorted, tile_expert, tile_valid, w_gu, w_down, e_off, tm, tf):
    p, d = h_sorted.shape
    d_ff = w_gu.shape[2] // 2
    nf = d_ff // tf

    def w_map(col0):
        def index(i, f, te, tv):
            return (e_off + te[i], 0, col0 + jnp.where(tv[i] > 0, f, nf - 1))
        return index

    def wd_map(i, f, te, tv):
        return (e_off + te[i], jnp.where(tv[i] > 0, f, nf - 1), 0)

    grid_spec = pltpu.PrefetchScalarGridSpec(
        num_scalar_prefetch=2,
        grid=(p // tm, nf),
        in_specs=[pl.BlockSpec((tm, d), lambda i, f, te, tv: (i, 0)),
                  pl.BlockSpec((1, d, tf), w_map(0)),
                  pl.BlockSpec((1, d, tf), w_map(nf)),
                  pl.BlockSpec((1, tf, d), wd_map)],
        out_specs=pl.BlockSpec((tm, d), lambda i, f, te, tv: (i, 0)),
        scratch_shapes=[pltpu.VMEM((tm, d), BF16)])
    return pl.pallas_call(
        _moe_ffn_kernel,
        grid_spec=grid_spec,
        out_shape=jax.ShapeDtypeStruct((p, d), F32),
        compiler_params=_cparams(2),
        name="moe_ffn",
    )(tile_expert, tile_valid, h_sorted, w_gu, w_gu, w_down)


def _moe_combine_kernel(*refs):
    idx_refs = refs[:TOP_E]
    y_hbm, x_ref, gt_ref, rt_ref, o_ref = refs[TOP_E:TOP_E + 5]
    bufs = refs[TOP_E + 5:2 * TOP_E + 5]
    sem = refs[2 * TOP_E + 5]
    rows = x_ref.shape[1]

    def start(r, c):
        for j in range(TOP_E):
            _row_copy(y_hbm, idx_refs[j][0, 0, r], bufs[j], r, sem.at[j]).start(priority=j % DMA_QUEUES)
        return c
    lax.fori_loop(0, rows, start, 0)

    def wait(r, c):
        for j in range(TOP_E):
            _row_copy(y_hbm, idx_refs[j][0, 0, r], bufs[j], r, sem.at[j]).wait()
        return c
    lax.fori_loop(0, rows, wait, 0)
    route = rt_ref[0]
    y = bufs[0][...] * route[:, TOP_E:TOP_E + 1]
    for j in range(1, TOP_E):
        y = y + bufs[j][...] * route[:, TOP_E + j:TOP_E + j + 1]
    o_ref[0] = x_ref[0] + gt_ref[0] * y


def _moe_combine(y_sorted, dest, route, x, gt, rows, per_row):
    G, T, d = x.shape
    nt = T // rows
    idx = [dest[:, j].reshape(G * nt, 1, rows) for j in range(TOP_E)]
    smem = pl.BlockSpec((1, 1, rows), lambda g, i: (g * nt + i, 0, 0), memory_space=pltpu.SMEM)
    row = pl.BlockSpec((1, rows, d), lambda g, i: (g, i, 0))
    return pl.pallas_call(
        _moe_combine_kernel,
        grid=(G, nt),
        in_specs=[smem] * TOP_E + [pl.BlockSpec(memory_space=pl.ANY), row, _mod_spec(rows, d, per_row),
                                   pl.BlockSpec((1, rows, LANES), lambda g, i: (g, i, 0))],
        out_specs=row,
        out_shape=jax.ShapeDtypeStruct((G, T, d), F32),
        scratch_shapes=[pltpu.VMEM((rows, d), F32) for _ in range(TOP_E)]
                       + [pltpu.SemaphoreType.DMA((TOP_E,))],
        compiler_params=_cparams(2),
        name="moe_combine",
    )(*idx, y_sorted, x, gt, route)


def _final_norm_kernel(x_ref, g_ref, o_ref):
    x = x_ref[0]
    inv = lax.rsqrt(jnp.mean(x * x, axis=-1, keepdims=True) + EPS)
    o_ref[0] = (x * inv) * g_ref[...]


def _final_norm(x, g, tm):
    G, T, d = x.shape
    row = pl.BlockSpec((1, tm, d), lambda g_, i: (g_, i, 0))
    return pl.pallas_call(
        _final_norm_kernel, grid=(G, T // tm),
        in_specs=[row, pl.BlockSpec(g.shape, lambda g_, i: (0, 0))], out_specs=row,
        out_shape=jax.ShapeDtypeStruct((G, T, d), F32),
        compiler_params=_cparams(2), name="final_norm",
    )(x, g)


def _largest_tile(n, cap):
    t = min(n, cap)
    while n % t:
        t //= 2
    return t


def kernel(x_prompt, x_sample, cache_k, cache_v, cache_kidx, state_conv, page_table, c_prompt, c_sample,
           w_ada, b_ada, g_norm_mix, g_norm_ffn, g_norm_final, w_attn_in, w_attn_out,
           w_conv_in, w_conv, w_conv_out, w_ffn_gate_up, w_ffn_down, w_router, b_router,
           w_moe_gate_up, w_moe_down):
    B, T, d = x_prompt.shape
    DB, Tn, _ = x_sample.shape
    assert Tn == 1
    depth = w_ada.shape[0]
    n_attn, n_pool, page, n_heads, head_dim = cache_k.shape
    idx_dim = cache_kidx.shape[-1]
    n_ih = (w_attn_in.shape[-1] - 3 * d - idx_dim) // (idx_dim + 1)
    n_pages = page_table.shape[1]
    past = n_pages * page
    n_experts = w_router.shape[-1]
    conv_w = w_conv.shape[1]
    topk_p = min(TOPK_MAX, T // 4)
    topk_s = min(TOPK_MAX, (past + Tn) // 4)

    tm_p = _largest_tile(T, 512)
    tm_a = _largest_tile(T, 256)
    tm_f = _largest_tile(T, 1024)
    tf = _largest_tile(w_ffn_gate_up.shape[-1] // 2, 512)
    pg = _largest_tile(n_pages, 8)
    tk = _largest_tile(T, 512)
    qb = _largest_tile(T, 256)
    rows_g = _largest_tile(T, 256)

    r = B + DB
    r_pad = -(-r // 8) * 8
    c_all = jnp.concatenate([c_prompt, c_sample, jnp.zeros((r_pad - r, d), F32)], axis=0)
    mod = _adaln(c_all, w_ada, b_ada)

    w_moe_gu = w_moe_gate_up.reshape((-1,) + w_moe_gate_up.shape[2:])
    w_moe_dn = w_moe_down.reshape((-1,) + w_moe_down.shape[2:])
    ck_t = jnp.transpose(cache_k, (0, 1, 3, 4, 2))
    cv_t = jnp.transpose(cache_v, (0, 1, 3, 4, 2))
    ckidx_t = jnp.swapaxes(cache_kidx, 2, 3)
    xp = x_prompt
    xs = x_sample.reshape(1, DB, d)

    outs = {n: [] for n in ("kp", "vp", "ip", "cp", "ks", "vs", "is", "cs")}
    for l in range(depth):
        j = l // 2
        mp = [mod[l, :B, i * d:(i + 1) * d].reshape(B, 1, d) for i in range(6)]
        ms = [mod[l, B:r, i * d:(i + 1) * d].reshape(1, DB, d) for i in range(6)]
        g_mix = g_norm_mix[l].reshape(1, d)
        g_ffn = g_norm_ffn[l].reshape(1, d)
        if l % 2 == 0:
            wi = w_attn_in[j].astype(BF16)
            nkw = idx_dim + n_ih
            wkw = jnp.pad(wi[:, 3 * d + n_ih * idx_dim:], ((0, 0), (0, LANES - nkw)))
            ws = (wi[:, :d], wi[:, d:2 * d], wi[:, 2 * d:3 * d],
                  wi[:, 3 * d:3 * d + n_ih * idx_dim], wkw)
            wo = w_attn_out[j].astype(BF16)
            _, q_t, kt, vt, kb, vb_t, qi, kw, kwt = _attn_in(xp, mp[0], mp[1], g_mix, ws, tm_a, False,
                                                             head_dim ** -0.5 * LOG2_E, idx_dim)
            widx = kw[:, :, idx_dim:nkw]
            kidx_t = kwt[:, :idx_dim, :]
            att_t = _prompt_attn(q_t, qi, widx, kidx_t.astype(BF16), kb, vb_t, qb=qb, tk=tk,
                                 topk=topk_p, n_heads=n_heads, n_ih=n_ih)
            xp = _proj_res(att_t, wo, xp, mp[2], tm_p, False, z_transposed=True)
            tokens_first = lambda a: jnp.moveaxis(a.reshape(B, n_heads, head_dim, T), 3, 1)
            outs["kp"].append(tokens_first(kt))
            outs["vp"].append(tokens_first(vt))
            outs["ip"].append(jnp.swapaxes(kidx_t, 1, 2))
            q, _, kt, vt, _, _, qi, kw, _ = _attn_in(xs, ms[0], ms[1], g_mix, ws, DB, True,
                                                     head_dim ** -0.5, idx_dim)
            k = kt[0].T
            v = vt[0].T
            kidx = kw[0, :, :idx_dim]
            widx = kw[0, :, idx_dim:nkw]
            sc_past = _sample_scores(page_table, qi.reshape(DB, n_ih, idx_dim),
                                     (widx * n_ih ** -0.5).reshape(DB, n_ih, 1), ckidx_t, j, pg)
            bias = _sample_select(sc_past, qi[0], kidx, widx, topk_s, n_ih)
            heads = lambda a: a.astype(F32).reshape(DB, n_heads, head_dim)
            att = _sample_attn(page_table, heads(q), heads(k), heads(v), bias, ck_t, cv_t, j, pg)
            xs = _proj_res(att.astype(BF16).reshape(1, DB, d), wo, xs, ms[2], DB, True)
            outs["ks"].append(k.reshape(DB, Tn, n_heads, head_dim))
            outs["vs"].append(v.reshape(DB, Tn, n_heads, head_dim))
            outs["is"].append(kidx.reshape(DB, Tn, idx_dim))
            hp = _norm(xp, mp[3], mp[4], g_ffn, tm_p, False)
            xp = _ffn(hp, xp, mp[5], w_ffn_gate_up, w_ffn_down, j, tm_f, tf, False)
            hs = _norm(xs, ms[3], ms[4], g_ffn, DB, True)
            xs = _ffn(hs, xs, ms[5], w_ffn_gate_up, w_ffn_down, j, DB, tf, True)
        else:
            wi = w_conv_in[j].astype(BF16)
            wb, wc, wx = wi[:, :d], wi[:, d:2 * d], wi[:, 2 * d:]
            wo = w_conv_out[j].astype(BF16)
            xp, st = _conv_prompt(xp, mp[0], mp[1], mp[2], g_mix, wb, wc, wx, w_conv[j], wo, tm_p)
            outs["cp"].append(st)
            xs2, ns = _conv_sample(xs[0], ms[0][0], ms[1][0], ms[2][0], g_mix,
                                   jnp.swapaxes(state_conv[j], 0, 1), wb, wc, wx, w_conv[j], wo)
            xs = xs2.reshape(1, DB, d)
            outs["cs"].append(jnp.swapaxes(ns, 0, 1))
            wr = jnp.pad(w_router[j], ((0, 0), (0, LANES - n_experts)))
            br = jnp.pad(b_router[j], (0, LANES - n_experts)).reshape(1, LANES)
            router = (wr, br, n_experts)
            hp, rp = _norm(xp, mp[3], mp[4], g_ffn, tm_p, False, router)
            hs, rs = _norm(xs, ms[3], ms[4], g_ffn, DB, True, router)
            h_all = jnp.concatenate([hp.reshape(B * T, d), hs.reshape(DB, d)], axis=0)
            route = jnp.concatenate([rp.reshape(B * T, LANES), rs.reshape(DB, LANES)], axis=0)
            src, dest, t_exp, t_val = _route_plan(route, n_experts, tm_f)
            h_sorted = _gather_rows(h_all, src, rows_g)
            y_sorted = _moe_ffn(h_sorted, t_exp, t_val, w_moe_gu, w_moe_dn, j * n_experts, tm_f, tf)
            xp = _moe_combine(y_sorted, dest[:B * T], rp, xp, mp[5], rows_g, False)
            xs = _moe_combine(y_sorted, dest[B * T:], rs, xs, ms[5], _largest_tile(DB, rows_g), True)

    gf = g_norm_final.reshape(1, d)
    y_prompt = _final_norm(xp, gf, tm_p)
    y_sample = _final_norm(xs, gf, DB).reshape(DB, Tn, d)
    st = lambda n: jnp.stack(outs[n])
    return (y_prompt, y_sample, st("kp"), st("vp"), st("ip"), st("cp"),
            st("ks"), st("vs"), st("is"), st("cs"))
```

```python
import functools

import jax
import jax.numpy as jnp
from jax import lax
from jax.experimental import pallas as pl
from jax.experimental.pallas import tpu as pltpu

F32 = jnp.float32
BF16 = jnp.bfloat16
I32 = jnp.int32

EPS = 1e-6
LOG2_E = 1.4426950408889634
TOPK_MAX = 256
TOP_E = 2
LANES = 128
NEG_BIAS = -1e30
INT_MIN = -(2 ** 31)
KEY_NEG_INF = -2139095041
KEY_POS_INF = 2139095040
VMEM_LIMIT = 56 * 1024 * 1024
SLAB_UNROLL = 4
DMA_QUEUES = 2
HEAD_GROUP = 8

def _cparams(n_axes):
    return pltpu.CompilerParams(dimension_semantics=("arbitrary",) * n_axes,
                                vmem_limit_bytes=VMEM_LIMIT)


def _norm_mod(x, g, sh, sc):
    inv = lax.rsqrt(jnp.mean(x * x, axis=-1, keepdims=True) + EPS)
    return ((x * inv) * g) * (1.0 + sc) + sh


def _silu(x):
    return x * (1.0 / (1.0 + jnp.exp(-x)))


def _float_key(s):
    bits = lax.bitcast_convert_type(s, I32)
    return jnp.where(bits < 0, bits ^ jnp.int32(0x7FFFFFFF), bits)


def _adaln_kernel(c_ref, w_ref, b_ref, o_ref):
    a = _silu(c_ref[...]).astype(BF16)
    o_ref[0] = jnp.dot(a, w_ref[0].astype(BF16), preferred_element_type=F32) + b_ref[0]


def _adaln(c_all, w_ada, b_ada):
    depth, d, n = w_ada.shape
    r = c_all.shape[0]
    tn = n // 4
    return pl.pallas_call(
        _adaln_kernel,
        grid=(depth, n // tn),
        in_specs=[pl.BlockSpec((r, d), lambda l, j: (0, 0)),
                  pl.BlockSpec((1, d, tn), lambda l, j: (l, 0, j)),
                  pl.BlockSpec((1, 1, tn), lambda l, j: (l, 0, j))],
        out_specs=pl.BlockSpec((1, r, tn), lambda l, j: (l, 0, j)),
        out_shape=jax.ShapeDtypeStruct((depth, r, n), F32),
        compiler_params=_cparams(2),
        name="adaln",
    )(c_all, w_ada, b_ada.reshape(depth, 1, n))


def _mod_spec(tm, d, per_row):
    if per_row:
        return pl.BlockSpec((1, tm, d), lambda g, i, *_: (g, i, 0))
    return pl.BlockSpec((1, 1, d), lambda g, i, *_: (g, 0, 0))


def _dot_nt(a, b):
    return lax.dot_general(a, b, (((1,), (1,)), ((), ())), preferred_element_type=F32)


def _attn_in_kernel(x_ref, sh_ref, sc_ref, g_ref, wq_ref, wqt_ref, wk_ref, wkt_ref, wvt_ref, wqi_ref,
                    wkw_ref, wkwt_ref, q_ref, qt_ref, kt_ref, vt_ref, kb_ref, vb_ref, qi_ref, kw_ref,
                    kwt_ref, *, q_scale, qi_scale):
    h = _norm_mod(x_ref[0], g_ref[...], sh_ref[0], sc_ref[0]).astype(BF16)
    q_ref[0] = (jnp.dot(h, wq_ref[...], preferred_element_type=F32) * q_scale).astype(BF16)
    qt_ref[0] = (_dot_nt(wqt_ref[...], h) * q_scale).astype(BF16)
    kt_ref[0] = _dot_nt(wkt_ref[...], h)
    kb_ref[0] = jnp.dot(h, wk_ref[...], preferred_element_type=F32).astype(BF16)
    vt = _dot_nt(wvt_ref[...], h)
    vt_ref[0] = vt
    vb_ref[0] = vt.astype(BF16)
    qi_ref[0] = (jnp.dot(h, wqi_ref[...], preferred_element_type=F32) * qi_scale).astype(BF16)
    kw_ref[0] = jnp.dot(h, wkw_ref[...], preferred_element_type=F32)
    kwt_ref[0] = _dot_nt(wkwt_ref[...], h)


def _attn_in(x, sh, sc, g, ws, tm, per_row, q_scale, idx_dim):
    G, T, d = x.shape
    wq, wk, wv, wqi, wkw = ws
    ws = (wq, wq.T, wk, wk.T, wv.T, wqi, wkw, wkw.T)
    nqi = wqi.shape[1]
    nkw = wkw.shape[1]
    full = lambda a: pl.BlockSpec(a.shape, lambda g_, i: (0,) * a.ndim)
    row = lambda n: pl.BlockSpec((1, tm, n), lambda g_, i: (g_, i, 0))
    col = lambda n: pl.BlockSpec((1, n, tm), lambda g_, i: (g_, 0, i))
    kern = functools.partial(_attn_in_kernel, q_scale=q_scale, qi_scale=idx_dim ** -0.5)
    return pl.pallas_call(
        kern,
        grid=(G, T // tm),
        in_specs=[row(d), _mod_spec(tm, d, per_row), _mod_spec(tm, d, per_row), full(g)]
                 + [full(w) for w in ws],
        out_specs=[row(d), col(d), col(d), col(d), row(d), col(d), row(nqi), row(nkw), col(nkw)],
        out_shape=[jax.ShapeDtypeStruct((G, T, d), BF16),
                   jax.ShapeDtypeStruct((G, d, T), BF16),
                   jax.ShapeDtypeStruct((G, d, T), F32),
                   jax.ShapeDtypeStruct((G, d, T), F32),
                   jax.ShapeDtypeStruct((G, T, d), BF16),
                   jax.ShapeDtypeStruct((G, d, T), BF16),
                   jax.ShapeDtypeStruct((G, T, nqi), BF16),
                   jax.ShapeDtypeStruct((G, T, nkw), F32),
                   jax.ShapeDtypeStruct((G, nkw, T), F32)],
        compiler_params=_cparams(2),
        name="attn_in",
    )(x, sh, sc, g, *ws)


def _count_groups(key_sc, n_slabs, groups, row_group, preds):
    def add(cnt, c, r0, pred):
        kk = key_sc[c, r0:r0 + row_group, :]
        return cnt + jnp.where(pred(kk, c), 1, 0).astype(I32)

    parts = []
    for r0, pred in zip(groups, preds):
        cnt = jnp.zeros((row_group, LANES), I32)
        if isinstance(n_slabs, int):
            for c in range(n_slabs):
                cnt = add(cnt, c, r0, pred)
        else:
            def body(cc, cnt, r0=r0, pred=pred):
                for u in range(SLAB_UNROLL):
                    cnt = add(cnt, cc * SLAB_UNROLL + u, r0, pred)
                return cnt
            cnt = lax.fori_loop(0, n_slabs // SLAB_UNROLL, body, cnt)
        parts.append(cnt)
    return [jnp.sum(p, axis=1, keepdims=True) for p in parts]


def _select_bias(key_sc, bias_sc, n_slabs, rows, row_group, topk):
    groups = list(range(0, rows, row_group))
    ng = len(groups)
    lane = lax.broadcasted_iota(I32, (row_group, LANES), 1)
    zeros = tuple(jnp.zeros((row_group, LANES), I32) for _ in groups)

    def bit_body(i, prefixes):
        bit = lax.shift_left(jnp.int32(1), 31 - i)
        trials = [p | bit for p in prefixes]
        preds = [lambda kk, c, ts=t ^ jnp.int32(INT_MIN): kk >= ts for t in trials]
        cnts = _count_groups(key_sc, n_slabs, groups, row_group, preds)
        return tuple(jnp.where(cnts[g] >= topk, trials[g], prefixes[g]) for g in range(ng))

    prefixes = lax.fori_loop(0, 32, bit_body, zeros)
    thrs = [p ^ jnp.int32(INT_MIN) for p in prefixes]
    n_gt = _count_groups(key_sc, n_slabs, groups, row_group,
                         [lambda kk, c, t=t: kk > t for t in thrs])
    n_eq = _count_groups(key_sc, n_slabs, groups, row_group,
                         [lambda kk, c, t=t: kk == t for t in thrs])
    needs = [topk - n for n in n_gt]
    tie = jnp.int32(0)
    for g in range(ng):
        tie = jnp.maximum(tie, jnp.max(jnp.where(n_eq[g] > needs[g], 1, 0)))

    def tie_cut():
        def cut_body(i, cuts):
            bit = lax.shift_left(jnp.int32(1), 30 - i)
            trials = [p | bit for p in cuts]
            preds = [lambda kk, c, t=thrs[g], tr=trials[g]: (kk == t) & (c * LANES + lane < tr)
                     for g in range(ng)]
            cnts = _count_groups(key_sc, n_slabs, groups, row_group, preds)
            return tuple(jnp.where(cnts[g] < needs[g], trials[g], cuts[g]) for g in range(ng))
        return lax.fori_loop(0, 31, cut_body, zeros)

    cuts = lax.cond(tie > 0, tie_cut,
                    lambda: tuple(jnp.full((row_group, LANES), 2 ** 31 - 1, I32) for _ in groups))

    def bias_body(c, carry):
        col = c * LANES + lane
        for g, r0 in enumerate(groups):
            kk = key_sc[c, r0:r0 + row_group, :]
            sel = (kk > thrs[g]) | ((kk == thrs[g]) & (col <= cuts[g]))
            fin = (kk > KEY_NEG_INF) & (kk < KEY_POS_INF)
            bias_sc[c, r0:r0 + row_group, :] = jnp.where(sel & fin, 0.0, NEG_BIAS).astype(F32)
        return carry
    lax.fori_loop(0, n_slabs, bias_body, 0)


def _prompt_attn_kernel(q_ref, qi_ref, w_ref, kT_ref, k_ref, v_ref, o_ref,
                        key_sc, bias_sc, m_sc, l_sc, acc_sc,
                        *, qb, tk, topk, n_heads, head_dim, n_ih, idx_dim, w_scale, row_group):
    qi_blk = pl.program_id(1)
    kj = pl.program_id(2)
    nk = pl.num_programs(2)
    n_valid = (qi_blk + 1) * qb
    n_chunks = (n_valid + tk - 1) // tk
    spc = tk // LANES

    @pl.when(kj == 0)
    def _indexer():
        qi = qi_ref[0]
        w = w_ref[0] * w_scale
        wcols = [w[:, h:h + 1] for h in range(n_ih)]
        row = qi_blk * qb + lax.broadcasted_iota(I32, (qb, tk), 0)
        lane = lax.broadcasted_iota(I32, (qb, tk), 1)

        def chunk_body(c, carry):
            kT = kT_ref[0, c]
            s = None
            for h in range(n_ih):
                d = jnp.dot(qi[:, h * idx_dim:(h + 1) * idx_dim], kT, preferred_element_type=F32)
                t = jnp.maximum(d, 0.0) * wcols[h]
                s = t if s is None else s + t
            key = jnp.where(c * tk + lane <= row, _float_key(s), jnp.int32(INT_MIN))
            for j in range(spc):
                key_sc[c * spc + j] = key[:, j * LANES:(j + 1) * LANES]
            return carry
        lax.fori_loop(0, n_chunks, chunk_body, 0)
        _select_bias(key_sc, bias_sc, n_chunks * spc, qb, row_group, topk)
        m_sc[...] = jnp.full(m_sc.shape, NEG_BIAS, F32)
        l_sc[...] = jnp.zeros(l_sc.shape, F32)
        acc_sc[...] = jnp.zeros(acc_sc.shape, F32)

    @pl.when(kj * tk < n_valid)
    def _attend():
        bias_t = jnp.concatenate([bias_sc[kj * spc + j] for j in range(spc)], axis=1).T
        hsl = lambda h: slice(h * head_dim, (h + 1) * head_dim)
        for h0 in range(0, n_heads, HEAD_GROUP):
            hh = range(h0, min(h0 + HEAD_GROUP, n_heads))
            ss = [jnp.dot(k_ref[0, h], q_ref[0, hsl(h), :], preferred_element_type=F32) + bias_t
                  for h in hh]
            m_new = [jnp.maximum(m_sc[h], jnp.max(s, axis=0, keepdims=True)) for h, s in zip(hh, ss)]
            alpha = [jnp.exp2(m_sc[h] - m) for h, m in zip(hh, m_new)]
            ps = [jnp.exp2(s - m) for s, m in zip(ss, m_new)]
            for i, h in enumerate(hh):
                l_sc[h] = alpha[i] * l_sc[h] + jnp.sum(ps[i], axis=0, keepdims=True)
                pv = jnp.dot(v_ref[0, hsl(h), :], ps[i].astype(BF16), preferred_element_type=F32)
                acc_sc[h] = acc_sc[h] * alpha[i] + pv
                m_sc[h] = m_new[i]

    @pl.when(kj == nk - 1)
    def _finish():
        for h in range(n_heads):
            o_ref[0, h * head_dim:(h + 1) * head_dim, :] = (
                acc_sc[h] * (1.0 / l_sc[h])).astype(o_ref.dtype)


def _prompt_attn(q_t, qi, w, kT, kb, vb_t, *, qb, tk, topk, n_heads, n_ih):
    B, d, T = q_t.shape
    head_dim = d // n_heads
    idx_dim = qi.shape[2] // n_ih
    nq, nk = T // qb, T // tk
    assert qb >= topk and tk % (LANES * SLAB_UNROLL) == 0 and qb <= tk
    kT = kT.reshape(B, idx_dim, nk, tk).transpose(0, 2, 1, 3)

    def last_tile(i, j):
        return jnp.minimum(j, ((i + 1) * qb - 1) // tk)

    kern = functools.partial(_prompt_attn_kernel, qb=qb, tk=tk, topk=topk, n_heads=n_heads,
                             head_dim=head_dim, n_ih=n_ih, idx_dim=idx_dim,
                             w_scale=n_ih ** -0.5, row_group=64)
    return pl.pallas_call(
        kern,
        grid=(B, nq, nk),
        in_specs=[pl.BlockSpec((1, d, qb), lambda b, i, j: (b, 0, i)),
                  pl.BlockSpec((1, qb, qi.shape[2]), lambda b, i, j: (b, i, 0)),
                  pl.BlockSpec((1, qb, w.shape[2]), lambda b, i, j: (b, i, 0)),
                  pl.BlockSpec((1, nk, idx_dim, tk), lambda b, i, j: (b, 0, 0, 0)),
                  pl.BlockSpec((1, n_heads, tk, head_dim), lambda b, i, j: (b, 0, last_tile(i, j), 0)),
                  pl.BlockSpec((1, d, tk), lambda b, i, j: (b, 0, last_tile(i, j)))],
        out_specs=pl.BlockSpec((1, d, qb), lambda b, i, j: (b, 0, i)),
        out_shape=jax.ShapeDtypeStruct((B, d, T), BF16),
        scratch_shapes=[pltpu.VMEM((T // LANES, qb, LANES), I32),
                        pltpu.VMEM((T // LANES, qb, LANES), F32),
                        pltpu.VMEM((n_heads, 1, qb), F32),
                        pltpu.VMEM((n_heads, 1, qb), F32),
                        pltpu.VMEM((n_heads, head_dim, qb), F32)],
        compiler_params=_cparams(3),
        name="prompt_attn",
    )(q_t, qi, w, kT, kb.reshape(B, T, n_heads, head_dim).transpose(0, 2, 1, 3), vb_t)


def _sample_scores_kernel(pt_ref, qi_ref, w_ref, *rest, pg):
    o_ref = rest[pg]
    qi = qi_ref[0]
    w = w_ref[0]
    for p in range(pg):
        kp = rest[p][0, 0].astype(BF16).astype(F32)
        d = jnp.dot(qi.astype(F32), kp, preferred_element_type=F32)
        o_ref[0, 0, p:p + 1, :] = jnp.sum(jnp.maximum(d, 0.0) * w, axis=0, keepdims=True)


def _sample_scores(page_table, qi3, w3, cache_kidx_t, layer, pg):
    DB, n_pages = page_table.shape
    _, _, idx_dim, page = cache_kidx_t.shape
    n_ih = qi3.shape[1]
    page_specs = [
        pl.BlockSpec((1, 1, idx_dim, page),
                     lambda b, j, pt, p=p: (layer, pt[b, j * pg + p], 0, 0))
        for p in range(pg)]
    grid_spec = pltpu.PrefetchScalarGridSpec(
        num_scalar_prefetch=1,
        grid=(DB, n_pages // pg),
        in_specs=[pl.BlockSpec((1, n_ih, idx_dim), lambda b, j, pt: (b, 0, 0)),
                  pl.BlockSpec((1, n_ih, 1), lambda b, j, pt: (b, 0, 0))] + page_specs,
        out_specs=pl.BlockSpec((1, 1, pg, page), lambda b, j, pt: (b, j, 0, 0)))
    out = pl.pallas_call(
        functools.partial(_sample_scores_kernel, pg=pg),
        grid_spec=grid_spec,
        out_shape=jax.ShapeDtypeStruct((DB, n_pages // pg, pg, page), F32),
        compiler_params=_cparams(2),
        name="sample_scores",
    )(page_table, qi3, w3, *([cache_kidx_t] * pg))
    return out.reshape(DB, n_pages * page)


def _sample_select_kernel(sp_ref, qi_ref, kn_ref, w_ref, bias_ref, key_sc, *, topk, n_ih, idx_dim,
                          w_scale, row_group):
    DB, past = sp_ref.shape
    n_past = past // LANES
    qi = qi_ref[...].astype(F32)
    kn = kn_ref[...]
    w = w_ref[...] * w_scale
    s_new = None
    for h in range(n_ih):
        d = jnp.sum(qi[:, h * idx_dim:(h + 1) * idx_dim] * kn, axis=1, keepdims=True)
        t = jnp.maximum(d, 0.0) * w[:, h:h + 1]
        s_new = t if s_new is None else s_new + t
    for c in range(n_past):
        key_sc[c] = _float_key(sp_ref[:, c * LANES:(c + 1) * LANES])
    lane = lax.broadcasted_iota(I32, (DB, LANES), 1)
    key_sc[n_past] = jnp.where(lane == 0, _float_key(jnp.broadcast_to(s_new, (DB, LANES))),
                               jnp.int32(INT_MIN))
    _select_bias(key_sc, bias_ref, n_past + 1, DB, row_group, topk)


def _sample_select(scores_past, qi, kn, w, topk, n_ih):
    DB, past = scores_past.shape
    idx_dim = kn.shape[1]
    n_slabs = past // LANES + 1
    kern = functools.partial(_sample_select_kernel, topk=topk, n_ih=n_ih, idx_dim=idx_dim,
                             w_scale=n_ih ** -0.5, row_group=min(64, DB))
    return pl.pallas_call(
        kern,
        out_shape=jax.ShapeDtypeStruct((n_slabs, DB, LANES), F32),
        scratch_shapes=[pltpu.VMEM((n_slabs, DB, LANES), I32)],
        compiler_params=pltpu.CompilerParams(vmem_limit_bytes=VMEM_LIMIT),
        name="sample_select",
    )(scores_past, qi, kn, w)


def _sample_attn_kernel(pt_ref, q_ref, kn_ref, vn_ref, bn_ref, bias_ref, *rest, pg, n_heads, head_dim):
    k_refs = rest[:pg]
    v_refs = rest[pg:2 * pg]
    o_ref = rest[2 * pg]
    m_sc, l_sc, acc_sc = rest[2 * pg + 1:]
    jg = pl.program_id(1)
    page = k_refs[0].shape[4]
    q = q_ref[0]

    def update(k_tiles, v_tiles, bias):
        s = jnp.concatenate([jnp.sum(kt * q, axis=1) for kt in k_tiles], axis=1) + bias
        m_prev = m_sc[...]
        m_new = jnp.maximum(m_prev, jnp.max(s, axis=1, keepdims=True))
        alpha = jnp.exp(m_prev - m_new)
        pr = jnp.exp(s - m_new)
        l_sc[...] = alpha * l_sc[...] + jnp.sum(pr, axis=1, keepdims=True)
        m_sc[...] = m_new
        for h in range(n_heads):
            acc = acc_sc[h] * alpha[h:h + 1, :]
            for i, vt in enumerate(v_tiles):
                acc = acc + vt[h] * pr[h:h + 1, i * LANES:(i + 1) * LANES]
            acc_sc[h] = acc

    @pl.when(jg == 0)
    def _init():
        m_sc[...] = jnp.full(m_sc.shape, NEG_BIAS, F32)
        l_sc[...] = jnp.zeros(l_sc.shape, F32)
        acc_sc[...] = jnp.zeros(acc_sc.shape, F32)
        update([kn_ref[0]], [vn_ref[0]], bn_ref[0])

    for p0 in range(0, pg, 2):
        tiles = range(p0, min(p0 + 2, pg))
        update([k_refs[p][0, 0] for p in tiles], [v_refs[p][0, 0] for p in tiles],
               bias_ref[0][:, p0 * page:(p0 + len(tiles)) * page])

    @pl.when(jg == pl.num_programs(1) - 1)
    def _finish():
        inv = 1.0 / l_sc[...]
        for h in range(n_heads):
            o_ref[0, h] = jnp.sum(acc_sc[h], axis=1, keepdims=True) * inv[h:h + 1, :]


def _sample_attn(page_table, q, kn, vn, bias, cache_k_t, cache_v_t, layer, pg):
    DB, n_pages = page_table.shape
    _, _, n_heads, head_dim, page = cache_k_t.shape
    n_slabs = bias.shape[0]
    past = n_pages * page
    assert page == LANES and n_slabs * LANES == past + LANES
    bias_rows = bias.transpose(1, 0, 2).reshape(DB, 1, n_slabs * LANES)
    lanes = lambda a: jnp.broadcast_to(a[..., None], a.shape + (LANES,))
    blk = pg * page
    page_spec = lambda p: pl.BlockSpec(
        (1, 1, n_heads, head_dim, page), lambda b, j, pt, p=p: (layer, pt[b, j * pg + p], 0, 0, 0))
    head = pl.BlockSpec((1, n_heads, head_dim, LANES), lambda b, j, pt: (b, 0, 0, 0))
    grid_spec = pltpu.PrefetchScalarGridSpec(
        num_scalar_prefetch=1,
        grid=(DB, n_pages // pg),
        in_specs=[head, head, head,
                  pl.BlockSpec((1, 1, LANES), lambda b, j, pt: (b, 0, past // LANES)),
                  pl.BlockSpec((1, 1, blk), lambda b, j, pt: (b, 0, j))]
                 + [page_spec(p) for p in range(pg)] * 2,
        out_specs=pl.BlockSpec((1, n_heads, head_dim, 1), lambda b, j, pt: (b, 0, 0, 0)),
        scratch_shapes=[pltpu.VMEM((n_heads, 1), F32),
                        pltpu.VMEM((n_heads, 1), F32),
                        pltpu.VMEM((n_heads, head_dim, LANES), F32)])
    out = pl.pallas_call(
        functools.partial(_sample_attn_kernel, pg=pg, n_heads=n_heads, head_dim=head_dim),
        grid_spec=grid_spec,
        out_shape=jax.ShapeDtypeStruct((DB, n_heads, head_dim, 1), F32),
        compiler_params=_cparams(2),
        name="sample_attn",
    )(page_table, lanes(q), lanes(kn), lanes(vn), bias_rows, bias_rows,
      *([cache_k_t] * pg), *([cache_v_t] * pg))
    return out.reshape(DB, n_heads * head_dim)


def _proj_res_kernel(z_ref, w_ref, x_ref, gt_ref, o_ref, *, z_transposed):
    z = z_ref[0]
    if z_transposed:
        z = z.astype(F32).T.astype(BF16)
    y = jnp.dot(z, w_ref[...], preferred_element_type=F32)
    o_ref[0] = x_ref[0] + gt_ref[0] * y


def _proj_res(z, w, x, gt, tm, per_row, z_transposed=False):
    G, T, d = x.shape
    row = pl.BlockSpec((1, tm, d), lambda g, i: (g, i, 0))
    z_spec = pl.BlockSpec((1, d, tm), lambda g, i: (g, 0, i)) if z_transposed else row
    return pl.pallas_call(
        functools.partial(_proj_res_kernel, z_transposed=z_transposed),
        grid=(G, T // tm),
        in_specs=[z_spec, pl.BlockSpec(w.shape, lambda g, i: (0, 0)), row, _mod_spec(tm, d, per_row)],
        out_specs=row,
        out_shape=jax.ShapeDtypeStruct((G, T, d), F32),
        compiler_params=_cparams(2),
        name="proj_res",
    )(z, w, x, gt)


def _conv_prompt_kernel(x_ref, sh_ref, sc_ref, gt_ref, g_ref, wb_ref, wc_ref, wx_ref, wcv_ref, wo_ref,
                        o_ref, st_ref, carry_sc, *, conv_w):
    i = pl.program_id(1)
    tm = x_ref.shape[1]
    x = x_ref[0]
    h = _norm_mod(x, g_ref[...], sh_ref[0], sc_ref[0]).astype(BF16)
    gb = jnp.dot(h, wb_ref[...], preferred_element_type=F32)
    u = (jnp.dot(h, wc_ref[...], preferred_element_type=F32)
         * jnp.dot(h, wx_ref[...], preferred_element_type=F32))

    @pl.when(i == 0)
    def _zero_state():
        carry_sc[...] = jnp.zeros(carry_sc.shape, F32)

    rows = lax.broadcasted_iota(I32, u.shape, 0)
    conv = u * wcv_ref[conv_w - 1:conv_w, :]
    for back in range(1, conv_w):
        shifted = pltpu.roll(u, back, axis=0)
        for r in range(back):
            prev = carry_sc[conv_w - 1 - back + r:conv_w - back + r, :]
            shifted = jnp.where(rows == r, prev, shifted)
        conv = conv + shifted * wcv_ref[conv_w - 1 - back:conv_w - back, :]
    carry_sc[...] = u[tm - (conv_w - 1):, :]
    st_ref[0] = u[tm - (conv_w - 1):, :]
    y = jnp.dot((gb * conv).astype(BF16), wo_ref[...], preferred_element_type=F32)
    o_ref[0] = x + gt_ref[0] * y


def _conv_prompt(x, sh, sc, gt, g, wb, wc, wx, wcv, wo, tm):
    B, T, d = x.shape
    conv_w = wcv.shape[0]
    row = pl.BlockSpec((1, tm, d), lambda b, i: (b, i, 0))
    full = lambda a: pl.BlockSpec(a.shape, lambda b, i: (0,) * a.ndim)
    mod = _mod_spec(tm, d, False)
    return pl.pallas_call(
        functools.partial(_conv_prompt_kernel, conv_w=conv_w),
        grid=(B, T // tm),
        in_specs=[row, mod, mod, mod, full(g), full(wb), full(wc), full(wx), full(wcv), full(wo)],
        out_specs=[row, pl.BlockSpec((1, conv_w - 1, d), lambda b, i: (b, 0, 0))],
        out_shape=[jax.ShapeDtypeStruct((B, T, d), F32),
                   jax.ShapeDtypeStruct((B, conv_w - 1, d), F32)],
        scratch_shapes=[pltpu.VMEM((conv_w - 1, d), F32)],
        compiler_params=_cparams(2),
        name="conv_prompt",
    )(x, sh, sc, gt, g, wb, wc, wx, wcv, wo)


def _conv_sample_kernel(x_ref, sh_ref, sc_ref, gt_ref, g_ref, st_ref, wb_ref, wc_ref, wx_ref, wcv_ref,
                        wo_ref, o_ref, ns_ref, *, conv_w):
    x = x_ref[...]
    h = _norm_mod(x, g_ref[...], sh_ref[...], sc_ref[...]).astype(BF16)
    gb = jnp.dot(h, wb_ref[...], preferred_element_type=F32)
    u = (jnp.dot(h, wc_ref[...], preferred_element_type=F32)
         * jnp.dot(h, wx_ref[...], preferred_element_type=F32))
    conv = u * wcv_ref[conv_w - 1:conv_w, :]
    for j in range(conv_w - 1):
        conv = conv + st_ref[j] * wcv_ref[j:j + 1, :]
    for j in range(conv_w - 2):
        ns_ref[j] = st_ref[j + 1]
    ns_ref[conv_w - 2] = u
    y = jnp.dot((gb * conv).astype(BF16), wo_ref[...], preferred_element_type=F32)
    o_ref[...] = x + gt_ref[...] * y


def _conv_sample(x, sh, sc, gt, g, state, wb, wc, wx, wcv, wo):
    DB, d = x.shape
    conv_w = wcv.shape[0]
    return pl.pallas_call(
        functools.partial(_conv_sample_kernel, conv_w=conv_w),
        out_shape=[jax.ShapeDtypeStruct((DB, d), F32),
                   jax.ShapeDtypeStruct((conv_w - 1, DB, d), F32)],
        compiler_params=pltpu.CompilerParams(vmem_limit_bytes=VMEM_LIMIT),
        name="conv_sample",
    )(x, sh, sc, gt, g, state, wb, wc, wx, wcv, wo)


def _norm_kernel(x_ref, sh_ref, sc_ref, g_ref, h_ref):
    h_ref[0] = _norm_mod(x_ref[0], g_ref[...], sh_ref[0], sc_ref[0]).astype(BF16)


def _norm_router_kernel(x_ref, sh_ref, sc_ref, g_ref, wr_ref, br_ref, h_ref, cb_ref, *, n_experts):
    h = _norm_mod(x_ref[0], g_ref[...], sh_ref[0], sc_ref[0])
    h_ref[0] = h
    logits = jnp.dot(h, wr_ref[...], preferred_element_type=F32,
                     precision=lax.Precision.HIGHEST) + br_ref[...]
    lane = lax.broadcasted_iota(I32, logits.shape, 1)
    ninf = jnp.float32(-jnp.inf)
    logits = jnp.where(lane < n_experts, logits, ninf)
    m1 = jnp.max(logits, axis=1, keepdims=True)
    i1 = jnp.min(jnp.where(logits == m1, lane, LANES), axis=1, keepdims=True)
    rest = jnp.where(lane == i1, ninf, logits)
    m2 = jnp.max(rest, axis=1, keepdims=True)
    i2 = jnp.min(jnp.where(rest == m2, lane, LANES), axis=1, keepdims=True)
    e2 = jnp.exp(m2 - m1)
    inv = 1.0 / (1.0 + e2)
    cb_ref[0] = jnp.where(lane == 0, i1.astype(F32), jnp.where(
        lane == 1, i2.astype(F32), jnp.where(lane == 2, inv, jnp.where(lane == 3, e2 * inv, 0.0))))


def _norm(x, sh, sc, g, tm, per_row, router=None):
    G, T, d = x.shape
    row = pl.BlockSpec((1, tm, d), lambda g_, i: (g_, i, 0))
    full = lambda a: pl.BlockSpec(a.shape, lambda g_, i: (0,) * a.ndim)
    mod = _mod_spec(tm, d, per_row)
    if router is None:
        return pl.pallas_call(
            _norm_kernel, grid=(G, T // tm),
            in_specs=[row, mod, mod, full(g)], out_specs=row,
            out_shape=jax.ShapeDtypeStruct((G, T, d), BF16),
            compiler_params=_cparams(2), name="norm",
        )(x, sh, sc, g)
    wr, br, n_experts = router
    return pl.pallas_call(
        functools.partial(_norm_router_kernel, n_experts=n_experts), grid=(G, T // tm),
        in_specs=[row, mod, mod, full(g), full(wr), full(br)],
        out_specs=[row, pl.BlockSpec((1, tm, LANES), lambda g_, i: (g_, i, 0))],
        out_shape=[jax.ShapeDtypeStruct((G, T, d), F32),
                   jax.ShapeDtypeStruct((G, T, LANES), F32)],
        compiler_params=_cparams(2), name="norm_router",
    )(x, sh, sc, g, wr, br)


def _swiglu_step(h, wg_ref, wu_ref, wd_ref):
    gate = jnp.dot(h, wg_ref[0].astype(BF16), preferred_element_type=F32)
    up = jnp.dot(h, wu_ref[0].astype(BF16), preferred_element_type=F32)
    a = _silu(gate) * up
    return jnp.dot(a.astype(BF16), wd_ref[0].astype(BF16), preferred_element_type=F32)


def _ffn_kernel(h_ref, x_ref, gt_ref, wg_ref, wu_ref, wd_ref, o_ref, acc_sc):
    f = pl.program_id(2)

    @pl.when(f == 0)
    def _zero():
        acc_sc[...] = jnp.zeros(acc_sc.shape, F32)

    acc_sc[...] += _swiglu_step(h_ref[0], wg_ref, wu_ref, wd_ref)

    @pl.when(f == pl.num_programs(2) - 1)
    def _finish():
        o_ref[0] = x_ref[0] + gt_ref[0] * acc_sc[...]


def _ffn(h, x, gt, w_gu, w_down, e, tm, tf, per_row):
    G, T, d = x.shape
    d_ff = w_gu.shape[2] // 2
    nf = d_ff // tf
    row = pl.BlockSpec((1, tm, d), lambda g, i, f: (g, i, 0))
    return pl.pallas_call(
        _ffn_kernel,
        grid=(G, T // tm, nf),
        in_specs=[row, row, _mod_spec(tm, d, per_row),
                  pl.BlockSpec((1, d, tf), lambda g, i, f: (e, 0, f)),
                  pl.BlockSpec((1, d, tf), lambda g, i, f: (e, 0, nf + f)),
                  pl.BlockSpec((1, tf, d), lambda g, i, f: (e, f, 0))],
        out_specs=row,
        out_shape=jax.ShapeDtypeStruct((G, T, d), F32),
        scratch_shapes=[pltpu.VMEM((tm, d), F32)],
        compiler_params=_cparams(3),
        name="ffn",
    )(h, x, gt, w_gu, w_gu, w_down)


def _route_plan(route, n_experts, tm):
    n = route.shape[0]
    flat_e = route[:, :TOP_E].astype(I32).reshape(-1)
    onehot = (flat_e[:, None] == jnp.arange(n_experts, dtype=I32)[None, :]).astype(I32)
    rank = jnp.sum((jnp.cumsum(onehot, axis=0) - onehot) * onehot, axis=1)
    counts = jnp.sum(onehot, axis=0)
    padded = (counts + tm - 1) // tm * tm
    ends = jnp.cumsum(padded)
    dest = (ends - padded)[flat_e] + rank
    n_tiles = (TOP_E * n + n_experts * (tm - 1) + tm - 1) // tm
    p_rows = n_tiles * tm
    src = jnp.zeros((p_rows,), I32).at[dest].set(jnp.arange(TOP_E * n, dtype=I32) // TOP_E)
    tile_start = jnp.arange(n_tiles, dtype=I32) * tm
    tile_expert = jnp.minimum(jnp.sum((ends[None, :] <= tile_start[:, None]).astype(I32), axis=1),
                              n_experts - 1)
    tile_valid = (tile_start < ends[-1]).astype(I32)
    return src, dest.reshape(n, TOP_E), tile_expert, tile_valid


def _row_copy(src_hbm, idx, dst, r, sem):
    return pltpu.make_async_copy(src_hbm.at[pl.ds(idx, 1)], dst.at[pl.ds(r, 1)], sem)


def _gather_rows_kernel(idx_ref, src_hbm, o_ref, sem):
    rows = o_ref.shape[0]

    def start(rr, c):
        for u in range(DMA_QUEUES):
            r = rr * DMA_QUEUES + u
            _row_copy(src_hbm, idx_ref[0, 0, r], o_ref, r, sem).start(priority=u)
        return c
    lax.fori_loop(0, rows // DMA_QUEUES, start, 0)

    def wait(r, c):
        _row_copy(src_hbm, idx_ref[0, 0, r], o_ref, r, sem).wait()
        return c
    lax.fori_loop(0, rows, wait, 0)


def _gather_rows(src, idx, rows):
    n, width = src.shape
    p = idx.shape[0]
    return pl.pallas_call(
        _gather_rows_kernel,
        grid=(p // rows,),
        in_specs=[pl.BlockSpec((1, 1, rows), lambda i: (i, 0, 0), memory_space=pltpu.SMEM),
                  pl.BlockSpec(memory_space=pl.ANY)],
        out_specs=pl.BlockSpec((rows, width), lambda i: (i, 0)),
        out_shape=jax.ShapeDtypeStruct((p, width), src.dtype),
        scratch_shapes=[pltpu.SemaphoreType.DMA(())],
        compiler_params=_cparams(1),
        name="gather_rows",
    )(idx.reshape(p // rows, 1, rows), src)


def _moe_ffn_kernel(te_ref, tv_ref, h_ref, wg_ref, wu_ref, wd_ref, o_ref, h_sc):
    i = pl.program_id(0)
    f = pl.program_id(1)

    @pl.when(f == 0)
    def _start():
        o_ref[...] = jnp.zeros(o_ref.shape, F32)
        h_sc[...] = h_ref[...].astype(BF16)

    @pl.when(tv_ref[i] > 0)
    def _compute():
        o_ref[...] += _swiglu_step(h_sc[...], wg_ref, wu_ref, wd_ref)


def _moe_ffn(h_sorted, tile_expert, tile_valid, w_gu, w_down, e_off, tm, tf):
    p, d = h_sorted.shape
    d_ff = w_gu.shape[2] // 2
    nf = d_ff // tf

    def w_map(col0):
        def index(i, f, te, tv):
            return (e_off + te[i], 0, col0 + jnp.where(tv[i] > 0, f, nf - 1))
        return index

    def wd_map(i, f, te, tv):
        return (e_off + te[i], jnp.where(tv[i] > 0, f, nf - 1), 0)

    grid_spec = pltpu.PrefetchScalarGridSpec(
        num_scalar_prefetch=2,
        grid=(p // tm, nf),
        in_specs=[pl.BlockSpec((tm, d), lambda i, f, te, tv: (i, 0)),
                  pl.BlockSpec((1, d, tf), w_map(0)),
                  pl.BlockSpec((1, d, tf), w_map(nf)),
                  pl.BlockSpec((1, tf, d), wd_map)],
        out_specs=pl.BlockSpec((tm, d), lambda i, f, te, tv: (i, 0)),
        scratch_shapes=[pltpu.VMEM((tm, d), BF16)])
    return pl.pallas_call(
        _moe_ffn_kernel,
        grid_spec=grid_spec,
        out_shape=jax.ShapeDtypeStruct((p, d), F32),
        compiler_params=_cparams(2),
        name="moe_ffn",
    )(tile_expert, tile_valid, h_sorted, w_gu, w_gu, w_down)


def _moe_combine_kernel(*refs):
    idx_refs = refs[:TOP_E]
    y_hbm, x_ref, gt_ref, rt_ref, o_ref = refs[TOP_E:TOP_E + 5]
    bufs = refs[TOP_E + 5:2 * TOP_E + 5]
    sem = refs[2 * TOP_E + 5]
    rows = x_ref.shape[1]

    def start(r, c):
        for j in range(TOP_E):
            _row_copy(y_hbm, idx_refs[j][0, 0, r], bufs[j], r, sem.at[j]).start(priority=j % DMA_QUEUES)
        return c
    lax.fori_loop(0, rows, start, 0)

    def wait(r, c):
        for j in range(TOP_E):
            _row_copy(y_hbm, idx_refs[j][0, 0, r], bufs[j], r, sem.at[j]).wait()
        return c
    lax.fori_loop(0, rows, wait, 0)
    route = rt_ref[0]
    y = bufs[0][...] * route[:, TOP_E:TOP_E + 1]
    for j in range(1, TOP_E):
        y = y + bufs[j][...] * route[:, TOP_E + j:TOP_E + j + 1]
    o_ref[0] = x_ref[0] + gt_ref[0] * y


def _moe_combine(y_sorted, dest, route, x, gt, rows, per_row):
    G, T, d = x.shape
    nt = T // rows
    idx = [dest[:, j].reshape(G * nt, 1, rows) for j in range(TOP_E)]
    smem = pl.BlockSpec((1, 1, rows), lambda g, i: (g * nt + i, 0, 0), memory_space=pltpu.SMEM)
    row = pl.BlockSpec((1, rows, d), lambda g, i: (g, i, 0))
    return pl.pallas_call(
        _moe_combine_kernel,
        grid=(G, nt),
        in_specs=[smem] * TOP_E + [pl.BlockSpec(memory_space=pl.ANY), row, _mod_spec(rows, d, per_row),
                                   pl.BlockSpec((1, rows, LANES), lambda g, i: (g, i, 0))],
        out_specs=row,
        out_shape=jax.ShapeDtypeStruct((G, T, d), F32),
        scratch_shapes=[pltpu.VMEM((rows, d), F32) for _ in range(TOP_E)]
                       + [pltpu.SemaphoreType.DMA((TOP_E,))],
        compiler_params=_cparams(2),
        name="moe_combine",
    )(*idx, y_sorted, x, gt, route)


def _final_norm_kernel(x_ref, g_ref, o_ref):
    x = x_ref[0]
    inv = lax.rsqrt(jnp.mean(x * x, axis=-1, keepdims=True) + EPS)
    o_ref[0] = (x * inv) * g_ref[...]


def _final_norm(x, g, tm):
    G, T, d = x.shape
    row = pl.BlockSpec((1, tm, d), lambda g_, i: (g_, i, 0))
    return pl.pallas_call(
        _final_norm_kernel, grid=(G, T // tm),
        in_specs=[row, pl.BlockSpec(g.shape, lambda g_, i: (0, 0))], out_specs=row,
        out_shape=jax.ShapeDtypeStruct((G, T, d), F32),
        compiler_params=_cparams(2), name="final_norm",
    )(x, g)


def _largest_tile(n, cap):
    t = min(n, cap)
    while n % t:
        t //= 2
    return t


def kernel(x_prompt, x_sample, cache_k, cache_v, cache_kidx, state_conv, page_table, c_prompt, c_sample,
           w_ada, b_ada, g_norm_mix, g_norm_ffn, g_norm_final, w_attn_in, w_attn_out,
           w_conv_in, w_conv, w_conv_out, w_ffn_gate_up, w_ffn_down, w_router, b_router,
           w_moe_gate_up, w_moe_down):
    B, T, d = x_prompt.shape
    DB, Tn, _ = x_sample.shape
    assert Tn == 1
    depth = w_ada.shape[0]
    n_attn, n_pool, page, n_heads, head_dim = cache_k.shape
    idx_dim = cache_kidx.shape[-1]
    n_ih = (w_attn_in.shape[-1] - 3 * d - idx_dim) // (idx_dim + 1)
    n_pages = page_table.shape[1]
    past = n_pages * page
    n_experts = w_router.shape[-1]
    conv_w = w_conv.shape[1]
    topk_p = min(TOPK_MAX, T // 4)
    topk_s = min(TOPK_MAX, (past + Tn) // 4)

    tm_p = _largest_tile(T, 512)
    tm_a = _largest_tile(T, 256)
    tm_f = _largest_tile(T, 1024)
    tf = _largest_tile(w_ffn_gate_up.shape[-1] // 2, 512)
    pg = _largest_tile(n_pages, 8)
    tk = _largest_tile(T, 512)
    qb = _largest_tile(T, 256)
    rows_g = _largest_tile(T, 512)

    r = B + DB
    r_pad = -(-r // 8) * 8
    c_all = jnp.concatenate([c_prompt, c_sample, jnp.zeros((r_pad - r, d), F32)], axis=0)
    mod = _adaln(c_all, w_ada, b_ada)

    w_moe_gu = w_moe_gate_up.reshape((-1,) + w_moe_gate_up.shape[2:])
    w_moe_dn = w_moe_down.reshape((-1,) + w_moe_down.shape[2:])
    ck_t = jnp.transpose(cache_k, (0, 1, 3, 4, 2))
    cv_t = jnp.transpose(cache_v, (0, 1, 3, 4, 2))
    ckidx_t = jnp.swapaxes(cache_kidx, 2, 3)
    xp = x_prompt
    xs = x_sample.reshape(1, DB, d)

    outs = {n: [] for n in ("kp", "vp", "ip", "cp", "ks", "vs", "is", "cs")}
    for l in range(depth):
        j = l // 2
        mp = [mod[l, :B, i * d:(i + 1) * d].reshape(B, 1, d) for i in range(6)]
        ms = [mod[l, B:r, i * d:(i + 1) * d].reshape(1, DB, d) for i in range(6)]
        g_mix = g_norm_mix[l].reshape(1, d)
        g_ffn = g_norm_ffn[l].reshape(1, d)
        if l % 2 == 0:
            wi = w_attn_in[j].astype(BF16)
            nkw = idx_dim + n_ih
            wkw = jnp.pad(wi[:, 3 * d + n_ih * idx_dim:], ((0, 0), (0, LANES - nkw)))
            ws = (wi[:, :d], wi[:, d:2 * d], wi[:, 2 * d:3 * d],
                  wi[:, 3 * d:3 * d + n_ih * idx_dim], wkw)
            wo = w_attn_out[j].astype(BF16)
            _, q_t, kt, vt, kb, vb_t, qi, kw, kwt = _attn_in(xp, mp[0], mp[1], g_mix, ws, tm_a, False,
                                                             head_dim ** -0.5 * LOG2_E, idx_dim)
            widx = kw[:, :, idx_dim:nkw]
            kidx_t = kwt[:, :idx_dim, :]
            att_t = _prompt_attn(q_t, qi, widx, kidx_t.astype(BF16), kb, vb_t, qb=qb, tk=tk,
                                 topk=topk_p, n_heads=n_heads, n_ih=n_ih)
            xp = _proj_res(att_t, wo, xp, mp[2], tm_p, False, z_transposed=True)
            tokens_first = lambda a: jnp.moveaxis(a.reshape(B, n_heads, head_dim, T), 3, 1)
            outs["kp"].append(tokens_first(kt))
            outs["vp"].append(tokens_first(vt))
            outs["ip"].append(jnp.swapaxes(kidx_t, 1, 2))
            q, _, kt, vt, _, _, qi, kw, _ = _attn_in(xs, ms[0], ms[1], g_mix, ws, DB, True,
                                                     head_dim ** -0.5, idx_dim)
            k = kt[0].T
            v = vt[0].T
            kidx = kw[0, :, :idx_dim]
            widx = kw[0, :, idx_dim:nkw]
            sc_past = _sample_scores(page_table, qi.reshape(DB, n_ih, idx_dim),
                                     (widx * n_ih ** -0.5).reshape(DB, n_ih, 1), ckidx_t, j, pg)
            bias = _sample_select(sc_past, qi[0], kidx, widx, topk_s, n_ih)
            heads = lambda a: a.astype(F32).reshape(DB, n_heads, head_dim)
            att = _sample_attn(page_table, heads(q), heads(k), heads(v), bias, ck_t, cv_t, j, pg)
            xs = _proj_res(att.astype(BF16).reshape(1, DB, d), wo, xs, ms[2], DB, True)
            outs["ks"].append(k.reshape(DB, Tn, n_heads, head_dim))
            outs["vs"].append(v.reshape(DB, Tn, n_heads, head_dim))
            outs["is"].append(kidx.reshape(DB, Tn, idx_dim))
            hp = _norm(xp, mp[3], mp[4], g_ffn, tm_p, False)
            xp = _ffn(hp, xp, mp[5], w_ffn_gate_up, w_ffn_down, j, tm_f, tf, False)
            hs = _norm(xs, ms[3], ms[4], g_ffn, DB, True)
            xs = _ffn(hs, xs, ms[5], w_ffn_gate_up, w_ffn_down, j, DB, tf, True)
        else:
            wi = w_conv_in[j].astype(BF16)
            wb, wc, wx = wi[:, :d], wi[:, d:2 * d], wi[:, 2 * d:]
            wo = w_conv_out[j].astype(BF16)
            xp, st = _conv_prompt(xp, mp[0], mp[1], mp[2], g_mix, wb, wc, wx, w_conv[j], wo, tm_p)
            outs["cp"].append(st)
            xs2, ns = _conv_sample(xs[0], ms[0][0], ms[1][0], ms[2][0], g_mix,
                                   jnp.swapaxes(state_conv[j], 0, 1), wb, wc, wx, w_conv[j], wo)
            xs = xs2.reshape(1, DB, d)
            outs["cs"].append(jnp.swapaxes(ns, 0, 1))
            wr = jnp.pad(w_router[j], ((0, 0), (0, LANES - n_experts)))
            br = jnp.pad(b_router[j], (0, LANES - n_experts)).reshape(1, LANES)
            router = (wr, br, n_experts)
            hp, rp = _norm(xp, mp[3], mp[4], g_ffn, tm_p, False, router)
            hs, rs = _norm(xs, ms[3], ms[4], g_ffn, DB, True, router)
            h_all = jnp.concatenate([hp.reshape(B * T, d), hs.reshape(DB, d)], axis=0)
            route = jnp.concatenate([rp.reshape(B * T, LANES), rs.reshape(DB, LANES)], axis=0)
            src, dest, t_exp, t_val = _route_plan(route, n_experts, tm_f)
            h_sorted = _gather_rows(h_all, src, rows_g)
            y_sorted = _moe_ffn(h_sorted, t_exp, t_val, w_moe_gu, w_moe_dn, j * n_experts, tm_f, tf)
            xp = _moe_combine(y_sorted, dest[:B * T], rp, xp, mp[5], rows_g, False)
            xs = _moe_combine(y_sorted, dest[B * T:], rs, xs, ms[5], _largest_tile(DB, rows_g), True)

    gf = g_norm_final.reshape(1, d)
    y_prompt = _final_norm(xp, gf, tm_p)
    y_sample = _final_norm(xs, gf, DB).reshape(DB, Tn, d)
    st = lambda n: jnp.stack(outs[n])
    return (y_prompt, y_sample, st("kp"), st("vp"), st("ip"), st("cp"),
            st("ks"), st("vs"), st("is"), st("cs"))
```
